```python
import functools
import jax
import jax.numpy as jnp
from jax import lax
import numpy as np

D_MODEL = 1024
BATCH = 8
SEQ = 2048
DEPTH = 1
DEC_BATCH = 128
DEC_SEQ = 8
PAST_LEN = 8192
PAGE_SIZE = 128

SB_HEADS = 8
SB_HEAD_DIM = D_MODEL // 16
SB_WIDTH = SB_HEADS * SB_HEAD_DIM
SB_SCALE = SB_HEAD_DIM ** -0.5
SB_BIAS_INIT = -8.0
POOL_WINDOWS = (2, 4, 8, 16)
POOL_GROUPS = len(POOL_WINDOWS)
POOL_GROUP_DIM = D_MODEL // 16
POOL_WIDTH = POOL_GROUPS * POOL_GROUP_DIM
POOL_BUF = max(POOL_WINDOWS) - 1
MEM_TOKENS = 256
MEM_HEADS = 4
MEM_HEAD_DIM = D_MODEL // 16
MEM_WIDTH = MEM_HEADS * MEM_HEAD_DIM
MEM_SCALE = MEM_HEAD_DIM ** -0.5
N_BRANCHES = 3
IN_WIDTH = 3 * SB_WIDTH + POOL_WIDTH + MEM_WIDTH
IN_SPLITS = (SB_WIDTH, 2 * SB_WIDTH, 3 * SB_WIDTH, 3 * SB_WIDTH + POOL_WIDTH)
D_FF = 2816
CONV_WIDTH = 3
Q_BLOCK = 128
NORM_EPS = 1e-6

kernel_name = "stickbreak_pool_memxattn_convffn_step"


def rmsnorm(x, g):
    xf = x.astype(jnp.float32)
    y = xf * lax.rsqrt(jnp.mean(xf * xf, axis=-1, keepdims=True) + NORM_EPS)
    return (y * g.astype(jnp.float32)).astype(x.dtype)


def sb_block(q, k, v, bias, mask, carry):
    z = jnp.einsum("bqhd,bkhd->bhqk", q, k, preferred_element_type=jnp.float32) * SB_SCALE
    z = z + bias.astype(jnp.float32)[None, :, None, None]
    log_keep = jnp.where(mask, jax.nn.log_sigmoid(-z), 0.0)
    log_keep_after = lax.cumsum(log_keep, axis=3, reverse=True) - log_keep + carry[..., None]
    a = jnp.where(mask, jnp.exp(jax.nn.log_sigmoid(z) + log_keep_after), 0.0)
    o = jnp.einsum("bhqk,bkhd->bqhd", a, v.astype(jnp.float32))
    return o, carry + jnp.sum(log_keep, axis=3)


def sb_prompt(q, k, v, bias):
    b, t, h, _ = q.shape
    outs = []
    for start in range(0, t, Q_BLOCK):
        end = start + Q_BLOCK
        qpos = jnp.arange(start, end)
        kpos = jnp.arange(end)
        mask = kpos[None, :] < qpos[:, None]
        carry0 = jnp.zeros((b, h, Q_BLOCK), jnp.float32)
        o, _ = sb_block(q[:, start:end], k[:, :end], v[:, :end], bias, mask, carry0)
        outs.append(o)
    return jnp.concatenate(outs, axis=1)


def sb_sample(q, k_new, v_new, bias, cache_k, cache_v, page_table):
    b, tq, h, _ = q.shape
    pos = jnp.arange(tq)
    mask_new = pos[None, :] < pos[:, None]
    o, carry = sb_block(q, k_new, v_new, bias, mask_new, jnp.zeros((b, h, tq), jnp.float32))
    mask_past = jnp.ones((tq, cache_k.shape[1]), bool)

    def step(state, phys):
        o_acc, c = state
        o_p, c = sb_block(q, cache_k[phys], cache_v[phys], bias, mask_past, c)
        return (o_acc + o_p, c), None

    (o, _), _ = lax.scan(step, (o, carry), page_table.T[::-1])
    return o


def pool_mix(u, buf, pos0, w_pool, pool_scale):
    b, t, c = u.shape
    ext = jnp.concatenate([buf, u], axis=1).astype(jnp.float32)
    cs = jnp.concatenate([jnp.zeros((b, 1, c), jnp.float32), jnp.cumsum(ext, axis=1)], axis=1)
    pos = pos0 + jnp.arange(t)
    means = []
    for g, w in enumerate(POOL_WINDOWS):
        sl = slice(g * POOL_GROUP_DIM, (g + 1) * POOL_GROUP_DIM)
        hi = cs[:, POOL_BUF + 1:POOL_BUF + 1 + t, sl]
        lo = cs[:, POOL_BUF + 1 - w:POOL_BUF + 1 - w + t, sl]
        cnt = jnp.minimum(w, pos + 1).astype(jnp.float32)
        means.append((hi - lo) / cnt[None, :, None])
    pooled = jnp.stack(means, axis=2) - u.astype(jnp.float32).reshape(b, t, POOL_GROUPS, POOL_GROUP_DIM)
    mixed = jnp.einsum("btgc,gcd->btgd", pooled, w_pool.astype(jnp.float32)).reshape(b, t, c)
    out = (mixed * pool_scale.astype(jnp.float32)).astype(u.dtype)
    return out, ext[:, -POOL_BUF:].astype(u.dtype)


def memory_kv(mem, g_mem, w_mem_kv):
    b, m, _ = mem.shape
    kv = rmsnorm(mem, g_mem) @ w_mem_kv
    k, v = jnp.split(kv, 2, axis=-1)
    return (k.reshape(b, m, MEM_HEADS, MEM_HEAD_DIM), v.reshape(b, m, MEM_HEADS, MEM_HEAD_DIM))


def mem_attend(q, k, v):
    s = jnp.einsum("bqhd,bmhd->bhqm", q, k, preferred_element_type=jnp.float32) * MEM_SCALE
    p = jax.nn.softmax(s, axis=-1)
    return jnp.einsum("bhqm,bmhd->bqhd", p, v.astype(jnp.float32))


def conv_ffn(h, buf, w_up, conv_w, conv_b, w_down):
    t = h.shape[1]
    up = h @ w_up
    ext = jnp.concatenate([buf, up], axis=1)
    conv = conv_b
    for j in range(CONV_WIDTH):
        conv = conv + ext[:, j:j + t] * conv_w[j]
    g, u = jnp.split(conv, 2, axis=-1)
    out = (jax.nn.gelu(g, approximate=False) * u) @ w_down
    return out, ext[:, -(CONV_WIDTH - 1):]


def trunk_layer(x, sb_attend, mem_k, mem_v, pool_buf, ffn_buf, pos0, p):
    b, t, _ = x.shape
    xn = rmsnorm(x, p["g_mix"])
    proj = xn @ p["w_in"]
    q_sb, k_sb, v_sb, u_pool, q_mem = jnp.split(proj, IN_SPLITS, axis=-1)
    q_sb = q_sb.reshape(b, t, SB_HEADS, SB_HEAD_DIM)
    k_sb = k_sb.reshape(b, t, SB_HEADS, SB_HEAD_DIM)
    v_sb = v_sb.reshape(b, t, SB_HEADS, SB_HEAD_DIM)
    o_sb = sb_attend(q_sb, k_sb, v_sb, p["sb_bias"]).reshape(b, t, SB_WIDTH).astype(x.dtype)
    o_pool, new_pool_buf = pool_mix(u_pool, pool_buf, pos0, p["w_pool"], p["pool_scale"])
    o_mem = mem_attend(q_mem.reshape(b, t, MEM_HEADS, MEM_HEAD_DIM), mem_k, mem_v)
    o_mem = o_mem.reshape(b, t, MEM_WIDTH).astype(x.dtype)
    gates = jax.nn.sigmoid((xn @ p["w_gate"] + p["b_gate"]).astype(jnp.float32))
    gates = gates.reshape(b, t, N_BRANCHES, D_MODEL)
    merged = (gates[:, :, 0] * (o_sb @ p["w_br_sb"]).astype(jnp.float32)
              + gates[:, :, 1] * (o_pool @ p["w_br_pool"]).astype(jnp.float32)
              + gates[:, :, 2] * (o_mem @ p["w_br_mem"]).astype(jnp.float32))
    h = x + merged.astype(x.dtype) @ p["w_out"]
    f, new_ffn_buf = conv_ffn(rmsnorm(h, p["g_ffn"]), ffn_buf, p["w_up"], p["conv_w"], p["conv_b"], p["w_down"])
    return h + f, k_sb, v_sb, new_pool_buf, new_ffn_buf


def setup_inputs(seed: int = 0) -> dict:
    key = jax.random.key(seed)
    ks = iter(jax.random.split(key, 40))
    f32 = jnp.float32
    L = DEPTH
    n_pages = PAST_LEN // PAGE_SIZE
    n_used = DEC_BATCH * n_pages
    n_pool = n_used + n_used // 4

    def nrm(shape, scale=1.0):
        return jax.random.normal(next(ks), shape, f32) * scale

    def gain(shape):
        return 1.0 + nrm(shape, 0.02)

    page_table = jax.random.permutation(next(ks), n_pool)[:n_used].reshape(DEC_BATCH, n_pages).astype(jnp.int32)
    return {
        "x_prompt": nrm((BATCH, SEQ, D_MODEL)),
        "x_sample": nrm((DEC_BATCH, DEC_SEQ, D_MODEL)),
        "cache_k": nrm((L, n_pool, PAGE_SIZE, SB_HEADS, SB_HEAD_DIM)),
        "cache_v": nrm((L, n_pool, PAGE_SIZE, SB_HEADS, SB_HEAD_DIM)),
        "cache_mem_k": nrm((L, DEC_BATCH, MEM_TOKENS, MEM_HEADS, MEM_HEAD_DIM)),
        "cache_mem_v": nrm((L, DEC_BATCH, MEM_TOKENS, MEM_HEADS, MEM_HEAD_DIM)),
        "state_pool": nrm((L, DEC_BATCH, POOL_BUF, POOL_WIDTH)),
        "state_ffn": nrm((L, DEC_BATCH, CONV_WIDTH - 1, 2 * D_FF)),
        "page_table": page_table,
        "mem_prompt": nrm((BATCH, MEM_TOKENS, D_MODEL)),
        "g_mix": gain((L, D_MODEL)),
        "w_in": nrm((L, D_MODEL, IN_WIDTH), D_MODEL ** -0.5),
        "sb_bias": SB_BIAS_INIT + nrm((L, SB_HEADS), 0.1),
        "w_gate": nrm((L, D_MODEL, N_BRANCHES * D_MODEL), D_MODEL ** -0.5),
        "b_gate": nrm((L, N_BRANCHES * D_MODEL), 0.02),
        "w_pool": nrm((L, POOL_GROUPS, POOL_GROUP_DIM, POOL_GROUP_DIM), POOL_GROUP_DIM ** -0.5),
        "pool_scale": gain((L, POOL_WIDTH)),
        "w_br_sb": nrm((L, SB_WIDTH, D_MODEL), SB_WIDTH ** -0.5),
        "w_br_pool": nrm((L, POOL_WIDTH, D_MODEL), POOL_WIDTH ** -0.5),
        "w_br_mem": nrm((L, MEM_WIDTH, D_MODEL), MEM_WIDTH ** -0.5),
        "w_out": nrm((L, D_MODEL, D_MODEL), D_MODEL ** -0.5),
        "g_mem": gain((L, D_MODEL)),
        "w_mem_kv": nrm((L, D_MODEL, 2 * MEM_WIDTH), D_MODEL ** -0.5),
        "g_ffn": gain((L, D_MODEL)),
        "w_up": nrm((L, D_MODEL, 2 * D_FF), D_MODEL ** -0.5),
        "conv_w": nrm((L, CONV_WIDTH, 2 * D_FF), CONV_WIDTH ** -0.5),
        "conv_b": nrm((L, 2 * D_FF), 0.02),
        "w_down": nrm((L, D_FF, D_MODEL), D_FF ** -0.5),
        "g_final": gain((D_MODEL,)),
    }


def reference(x_prompt, x_sample, cache_k, cache_v, cache_mem_k, cache_mem_v, state_pool, state_ffn,
              page_table, mem_prompt, g_mix, w_in, sb_bias, w_gate, b_gate, w_pool, pool_scale, w_br_sb,
              w_br_pool, w_br_mem, w_out, g_mem, w_mem_kv, g_ffn, w_up, conv_w, conv_b, w_down, g_final):
    past_len = page_table.shape[1] * cache_k.shape[2]
    hp, hs = x_prompt, x_sample
    nb = x_prompt.shape[0]
    k_p, v_p, k_s, v_s, mk_p, mv_p, pool_p, pool_s, ffn_p, ffn_s = ([] for _ in range(10))
    for l in range(DEPTH):
        p = {"g_mix": g_mix[l], "w_in": w_in[l], "sb_bias": sb_bias[l], "w_gate": w_gate[l],
             "b_gate": b_gate[l], "w_pool": w_pool[l], "pool_scale": pool_scale[l], "w_br_sb": w_br_sb[l],
             "w_br_pool": w_br_pool[l], "w_br_mem": w_br_mem[l], "w_out": w_out[l], "g_ffn": g_ffn[l],
             "w_up": w_up[l], "conv_w": conv_w[l], "conv_b": conv_b[l], "w_down": w_down[l]}
        mk, mv = memory_kv(mem_prompt, g_mem[l], w_mem_kv[l])
        pool0 = jnp.zeros((nb, POOL_BUF, POOL_WIDTH), hp.dtype)
        ffn0 = jnp.zeros((nb, CONV_WIDTH - 1, 2 * D_FF), hp.dtype)
        hp, kl, vl, pl, fl = trunk_layer(hp, sb_prompt, mk, mv, pool0, ffn0, 0, p)
        k_p.append(kl); v_p.append(vl); mk_p.append(mk); mv_p.append(mv); pool_p.append(pl); ffn_p.append(fl)
        sb_s = functools.partial(sb_sample, cache_k=cache_k[l], cache_v=cache_v[l], page_table=page_table)
        hs, kl, vl, pl, fl = trunk_layer(hs, sb_s, cache_mem_k[l], cache_mem_v[l], state_pool[l], state_ffn[l], past_len, p)
        k_s.append(kl); v_s.append(vl); pool_s.append(pl); ffn_s.append(fl)
    y_prompt = rmsnorm(hp, g_final)
    y_sample = rmsnorm(hs, g_final)
    return (y_prompt, y_sample, jnp.stack(k_p), jnp.stack(v_p), jnp.stack(k_s), jnp.stack(v_s),
            jnp.stack(mk_p), jnp.stack(mv_p), jnp.stack(pool_p), jnp.stack(pool_s),
            jnp.stack(ffn_p), jnp.stack(ffn_s))
```

```python
import functools

import jax
import jax.numpy as jnp
from jax import lax
from jax.experimental import pallas as pl
from jax.experimental.pallas import tpu as pltpu

F32 = jnp.float32
BF16 = jnp.bfloat16

NORM_EPS = 1e-6
LANES = 128
MXU_COLS = 256
HEAD_DIM = 64
POOL_WINDOWS = (2, 4, 8, 16)
POOL_BUF = max(POOL_WINDOWS) - 1
CONV_WIDTH = 3
KEY_BLOCK = 128
VMEM_LIMIT = 56 * 1024 * 1024


def _params(*sem):
    return pltpu.CompilerParams(dimension_semantics=sem, vmem_limit_bytes=VMEM_LIMIT)


def _rmsnorm(x, g):
    ms = jnp.mean(x * x, axis=-1, keepdims=True)
    return x * lax.rsqrt(ms + NORM_EPS) * g


def _div_pow2(x, n):
    assert n & (n - 1) == 0
    return lax.shift_right_logical(x, n.bit_length() - 1)


def _head_of(lane):
    return _div_pow2(lane, HEAD_DIM)


def _const_spec(a, grid_rank):
    index = {1: lambda i: (0,) * a.ndim, 2: lambda i, j: (0,) * a.ndim}[grid_rank]
    return pl.BlockSpec(a.shape, index, pipeline_mode=pl.Buffered(1))


def _dot(a, b):
    return jnp.dot(a, b, preferred_element_type=F32)


def _dot_nt(a, b):
    return lax.dot_general(a, b, (((1,), (1,)), ((), ())), preferred_element_type=F32)


def _norm_proj_kernel(x_ref, g_ref, w_ref, *out_refs, cols):
    xn = _rmsnorm(x_ref[...], g_ref[...]).astype(BF16)
    for o_ref, (off, width, scale) in zip(out_refs, cols):
        y = _dot(xn, w_ref[:, off:off + width])
        if scale != 1.0:
            y = y * scale
        o_ref[...] = y.astype(o_ref.dtype)


def _norm_proj(x, g, w, outs, tm):
    rows, d = x.shape
    n = w.shape[1]
    cols = tuple((off, width, scale) for off, width, _, scale in outs)
    return pl.pallas_call(
        functools.partial(_norm_proj_kernel, cols=cols),
        grid=(rows // tm,),
        in_specs=[pl.BlockSpec((tm, d), lambda i: (i, 0)),
                  pl.BlockSpec((1, d), lambda i: (0, 0)),
                  pl.BlockSpec((d, n), lambda i: (0, 0))],
        out_specs=[pl.BlockSpec((tm, width), lambda i: (i, 0)) for _, width, _, _ in outs],
        out_shape=[jax.ShapeDtypeStruct((rows, width), dt) for _, width, dt, _ in outs],
        compiler_params=_params("parallel"),
        name="norm_proj",
    )(x, g.reshape(1, d), w)


def _sb_block(qbd, kb, vb, bias, carry, tri, mask):
    z = _dot_nt(qbd, kb) + bias
    log_keep = jnp.minimum(-z, 0.0) - jnp.log1p(jnp.exp(-jnp.abs(z)))
    if mask is not None:
        log_keep = jnp.where(mask, log_keep, 0.0)
    sums = _dot(log_keep.astype(BF16), tri)
    a = jnp.exp(z + sums[:, :KEY_BLOCK] + carry)
    if mask is not None:
        a = jnp.where(mask, a, 0.0)
    return _dot(a.astype(BF16), vb), carry + sums[:, KEY_BLOCK:]


def _suffix_sum_matrix():
    r = lax.broadcasted_iota(jnp.int32, (KEY_BLOCK, 2 * KEY_BLOCK), 0)
    c = lax.broadcasted_iota(jnp.int32, (KEY_BLOCK, 2 * KEY_BLOCK), 1)
    return jnp.where((c >= KEY_BLOCK) | (r >= c), 1.0, 0.0).astype(BF16)


def _sb_prompt_kernel(q_ref, k_ref, v_ref, bias_ref, tri_ref, o_ref):
    tq = q_ref.shape[1]
    qi = pl.program_id(2)
    q2 = q_ref[0]
    lane = lax.broadcasted_iota(jnp.int32, (tq, LANES), 1)
    zero = jnp.zeros_like(q2)
    qbd = jnp.concatenate([jnp.where(lane < HEAD_DIM, q2, zero),
                           jnp.where(lane >= HEAD_DIM, q2, zero)], axis=0)
    bias = bias_ref[0]
    tri = tri_ref[...]
    row = lax.broadcasted_iota(jnp.int32, (2 * tq, KEY_BLOCK), 0)
    col = lax.broadcasted_iota(jnp.int32, (2 * tq, KEY_BLOCK), 1)
    causal = col < jnp.where(row >= tq, row - tq, row)

    start = pl.multiple_of(qi * tq, tq)
    acc, carry = _sb_block(qbd, k_ref[0, pl.ds(start, tq), :], v_ref[0, pl.ds(start, tq), :],
                           bias, jnp.zeros((2 * tq, LANES), F32), tri, causal)

    def earlier(i, state):
        acc, carry = state
        s = pl.multiple_of((qi - 1 - i) * KEY_BLOCK, KEY_BLOCK)
        o, carry = _sb_block(qbd, k_ref[0, pl.ds(s, KEY_BLOCK), :], v_ref[0, pl.ds(s, KEY_BLOCK), :],
                             bias, carry, tri, None)
        return acc + o, carry

    acc, _ = lax.fori_loop(0, qi, earlier, (acc, carry))
    o_ref[0] = jnp.where(lane < HEAD_DIM, acc[:tq], acc[tq:]).astype(o_ref.dtype)


def _sb_prompt(q, k, v, bias_rows, tri):
    b, t, width = q.shape
    tq = KEY_BLOCK
    pairs = width // LANES
    return pl.pallas_call(
        _sb_prompt_kernel,
        grid=(b, pairs, t // tq),
        in_specs=[pl.BlockSpec((1, tq, LANES), lambda bi, hp, qi: (bi, qi, hp)),
                  pl.BlockSpec((1, t, LANES), lambda bi, hp, qi: (bi, 0, hp)),
                  pl.BlockSpec((1, t, LANES), lambda bi, hp, qi: (bi, 0, hp)),
                  pl.BlockSpec((1, 2 * tq, LANES), lambda bi, hp, qi: (hp, 0, 0)),
                  pl.BlockSpec((KEY_BLOCK, 2 * KEY_BLOCK), lambda bi, hp, qi: (0, 0))],
        out_specs=pl.BlockSpec((1, tq, LANES), lambda bi, hp, qi: (bi, qi, hp)),
        out_shape=jax.ShapeDtypeStruct((b, t, width), BF16),
        compiler_params=_params("parallel", "parallel", "arbitrary"),
        name="sb_prompt",
    )(q, k, v, bias_rows, tri)


def _sb_sample_kernel(pt_ref, q_ref, kn_ref, vn_ref, bias_ref, tri_ref, *rest, pages_per_step):
    del pt_ref
    k_refs = rest[:pages_per_step]
    v_refs = rest[pages_per_step:2 * pages_per_step]
    o_ref, qbd_sc, carry_sc, acc_sc = rest[2 * pages_per_step:]
    tq, width = q_ref.shape[1], q_ref.shape[2]
    heads = width // HEAD_DIM
    m = heads * tq
    step = pl.program_id(1)
    bias = bias_ref[...]
    tri = tri_ref[...]

    @pl.when(step == 0)
    def _new_tokens():
        qt = jnp.concatenate([q_ref[0]] * heads, axis=0)
        row = lax.broadcasted_iota(jnp.int32, (m, width), 0)
        lane = lax.broadcasted_iota(jnp.int32, (m, width), 1)
        qbd = jnp.where(_div_pow2(row, tq) == _head_of(lane), qt, 0.0).astype(BF16)
        qbd_sc[...] = qbd
        pad = jnp.zeros((KEY_BLOCK - tq, width), F32)
        kn = jnp.concatenate([kn_ref[0], pad], axis=0).astype(BF16)
        vn = jnp.concatenate([vn_ref[0], pad], axis=0).astype(BF16)
        r = lax.broadcasted_iota(jnp.int32, (m, KEY_BLOCK), 0)
        c = lax.broadcasted_iota(jnp.int32, (m, KEY_BLOCK), 1)
        o, carry = _sb_block(qbd, kn, vn, bias, jnp.zeros((m, LANES), F32), tri,
                                 c < r - tq * _div_pow2(r, tq))
        acc_sc[...] = o
        carry_sc[...] = carry

    qbd = qbd_sc[...]
    for k_ref, v_ref in zip(k_refs, v_refs):
        o, carry = _sb_block(qbd, k_ref[0].astype(BF16), v_ref[0].astype(BF16),
                             bias, carry_sc[...], tri, None)
        acc_sc[...] += o
        carry_sc[...] = carry

    @pl.when(step == pl.num_programs(1) - 1)
    def _finish():
        lane = lax.broadcasted_iota(jnp.int32, (tq, LANES), 1)
        pieces = []
        for j in range(width // LANES):
            lo = acc_sc[2 * j * tq:(2 * j + 1) * tq, j * LANES:(j + 1) * LANES]
            hi = acc_sc[(2 * j + 1) * tq:(2 * j + 2) * tq, j * LANES:(j + 1) * LANES]
            pieces.append(jnp.where(lane < HEAD_DIM, lo, hi))
        o_ref[0] = jnp.concatenate(pieces, axis=1).astype(o_ref.dtype)


def _sb_sample(q, k_new, v_new, cache_k, cache_v, page_table, bias_rows, tri, pages_per_step):
    b, tq, width = q.shape
    n_pages = page_table.shape[1]
    m = (width // HEAD_DIM) * tq
    steps = n_pages // pages_per_step

    def page_spec(j):
        def index(bi, s, pt):
            return (pt[bi * n_pages + n_pages - 1 - (s * pages_per_step + j)], 0, 0)
        return pl.BlockSpec((1, KEY_BLOCK, width), index)

    tok = pl.BlockSpec((1, tq, width), lambda bi, s, pt: (bi, 0, 0))
    grid_spec = pltpu.PrefetchScalarGridSpec(
        num_scalar_prefetch=1,
        grid=(b, steps),
        in_specs=[tok, tok, tok,
                  pl.BlockSpec((m, LANES), lambda bi, s, pt: (0, 0)),
                  pl.BlockSpec((KEY_BLOCK, 2 * KEY_BLOCK), lambda bi, s, pt: (0, 0))]
                 + [page_spec(j) for _ in range(2) for j in range(pages_per_step)],
        out_specs=tok,
        scratch_shapes=[pltpu.VMEM((m, width), BF16),
                        pltpu.VMEM((m, LANES), F32),
                        pltpu.VMEM((m, width), F32)],
    )
    return pl.pallas_call(
        functools.partial(_sb_sample_kernel, pages_per_step=pages_per_step),
        grid_spec=grid_spec,
        out_shape=jax.ShapeDtypeStruct((b, tq, width), BF16),
        compiler_params=_params("parallel", "arbitrary"),
        name="sb_sample",
    )(page_table.reshape(-1), q, k_new, v_new, bias_rows, tri,
      *([cache_k] * pages_per_step), *([cache_v] * pages_per_step))


def _mem_attn_kernel(q_ref, k_ref, v_ref, o_ref):
    tm, width = q_ref.shape[1], q_ref.shape[2]
    heads = width // HEAD_DIM
    q = q_ref[0]
    lane = lax.broadcasted_iota(jnp.int32, (tm, width), 1)
    qbd = jnp.concatenate([jnp.where(_head_of(lane) == h, q, 0.0) for h in range(heads)],
                          axis=0).astype(BF16)
    s = _dot_nt(qbd, k_ref[0].astype(BF16))
    p = jnp.exp(s - jnp.max(s, axis=-1, keepdims=True))
    denom = jnp.sum(p, axis=-1, keepdims=True)
    o = _dot(p.astype(BF16), v_ref[0].astype(BF16)) / denom
    out = o[:tm]
    for h in range(1, heads):
        out = jnp.where(_head_of(lane) == h, o[h * tm:(h + 1) * tm], out)
    o_ref[0] = out.astype(o_ref.dtype)


def _mem_attn(q, k, v, tm):
    b, t, width = q.shape
    mem = k.shape[1]
    return pl.pallas_call(
        _mem_attn_kernel,
        grid=(b, t // tm),
        in_specs=[pl.BlockSpec((1, tm, width), lambda bi, i: (bi, i, 0)),
                  pl.BlockSpec((1, mem, width), lambda bi, i: (bi, 0, 0)),
                  pl.BlockSpec((1, mem, width), lambda bi, i: (bi, 0, 0))],
        out_specs=pl.BlockSpec((1, tm, width), lambda bi, i: (bi, i, 0)),
        out_shape=jax.ShapeDtypeStruct((b, t, width), BF16),
        compiler_params=_params("parallel", "parallel"),
        name="mem_attn",
    )(q, k, v)


def _window_sums(ext, t, axis):
    def cut(x, lo, n):
        return lax.slice_in_dim(x, lo, lo + n, axis=axis)

    length = ext.shape[axis]
    sums, cur, span = [], ext, 1
    for w in POOL_WINDOWS:
        while span < w:
            n = cur.shape[axis] - span
            cur = cut(cur, span, n) + cut(cur, 0, n)
            span *= 2
        sums.append(cut(cur, length - t - w + 1, t))
    return sums


def _pool_finish(sums, u, pos, wbd_ref, scale_ref, o_ref):
    lane = lax.broadcasted_iota(jnp.int32, u.shape, 1)
    total, window = sums[0], jnp.full(u.shape, POOL_WINDOWS[0], jnp.int32)
    for g in range(1, len(POOL_WINDOWS)):
        sel = _head_of(lane) == g
        total = jnp.where(sel, sums[g], total)
        window = jnp.where(sel, POOL_WINDOWS[g], window)
    count = jnp.minimum(window, pos + 1).astype(F32)
    pooled = total / count - u
    o_ref[...] = (_dot(pooled.astype(BF16), wbd_ref[...]) * scale_ref[...]).astype(o_ref.dtype).reshape(o_ref.shape)


def _pool_prompt_kernel(u_ref, halo_ref, wbd_ref, scale_ref, o_ref):
    tm = u_ref.shape[1]
    i = pl.program_id(1)
    u = u_ref[0]
    halo = jnp.where(i > 0, halo_ref[0], 0.0)
    ext = jnp.concatenate([halo, u], axis=0)
    pos = i * tm + lax.broadcasted_iota(jnp.int32, u.shape, 0)
    _pool_finish(_window_sums(ext, tm, 0), u, pos, wbd_ref, scale_ref, o_ref)


def _pool_prompt(u, wbd, scale, tm):
    b, t, width = u.shape
    halo = 16
    return pl.pallas_call(
        _pool_prompt_kernel,
        grid=(b, t // tm),
        in_specs=[pl.BlockSpec((1, tm, width), lambda bi, i: (bi, i, 0)),
                  pl.BlockSpec((1, halo, width), lambda bi, i: (bi, jnp.maximum(i * (tm // halo) - 1, 0), 0)),
                  pl.BlockSpec((width, width), lambda bi, i: (0, 0)),
                  pl.BlockSpec((1, width), lambda bi, i: (0, 0))],
        out_specs=pl.BlockSpec((1, tm, width), lambda bi, i: (bi, i, 0)),
        out_shape=jax.ShapeDtypeStruct((b, t, width), BF16),
        compiler_params=_params("parallel", "parallel"),
        name="pool_prompt",
    )(u, u, wbd, scale)


def _pool_sample_kernel(ext_ref, wbd_ref, scale_ref, o_ref, *, pos0):
    t, b, width = o_ref.shape
    ext = ext_ref[...]
    sums = [s.reshape(t * b, width) for s in _window_sums(ext, t, 0)]
    u = ext[POOL_BUF:].reshape(t * b, width)
    pos = pos0 + lax.broadcasted_iota(jnp.int32, (t, b, width), 0).reshape(t * b, width)
    _pool_finish(sums, u, pos, wbd_ref, scale_ref, o_ref)


def _pool_sample(ext, wbd, scale, pos0):
    rows, b, width = ext.shape
    t = rows - POOL_BUF
    return pl.pallas_call(
        functools.partial(_pool_sample_kernel, pos0=pos0),
        out_shape=jax.ShapeDtypeStruct((t, b, width), BF16),
        compiler_params=pltpu.CompilerParams(vmem_limit_bytes=VMEM_LIMIT),
        name="pool_sample",
    )(ext, wbd, scale)


def _merge_kernel(x_ref, sb_ref, pool_ref, mem_ref, g_ref, wg_ref, bg_ref,
                  wsb_ref, wpool_ref, wmem_ref, wout_ref, h_ref):
    d = x_ref.shape[1]
    x = x_ref[...]
    xn = _rmsnorm(x, g_ref[...]).astype(BF16)
    merged = None
    for j, (o_ref, w_ref) in enumerate(((sb_ref, wsb_ref), (pool_ref, wpool_ref), (mem_ref, wmem_ref))):
        gate = jax.nn.sigmoid(_dot(xn, wg_ref[:, j * d:(j + 1) * d]) + bg_ref[:, j * d:(j + 1) * d])
        term = gate * _dot(o_ref[...], w_ref[...])
        merged = term if merged is None else merged + term
    h_ref[...] = x + _dot(merged.astype(BF16), wout_ref[...])


def _merge(x, o_sb, o_pool, o_mem, g, w_gate, b_gate, w_sb, w_pool, w_mem, w_out, tm):
    rows, d = x.shape

    def rows_spec(a):
        return pl.BlockSpec((tm, a.shape[1]), lambda i: (i, 0))

    g2, bg2 = g.reshape(1, d), b_gate.reshape(1, -1)
    consts = (g2, w_gate, bg2, w_sb, w_pool, w_mem, w_out)
    return pl.pallas_call(
        _merge_kernel,
        grid=(rows // tm,),
        in_specs=[rows_spec(a) for a in (x, o_sb, o_pool, o_mem)] + [_const_spec(a, 1) for a in consts],
        out_specs=pl.BlockSpec((tm, d), lambda i: (i, 0)),
        out_shape=jax.ShapeDtypeStruct((rows, d), F32),
        compiler_params=_params("parallel"),
        name="merge",
    )(x, o_sb, o_pool, o_mem, *consts)


def _conv_ffn_kernel(h_ref, hist0_ref, g_ref, wup_ref, cw_ref, cb_ref, wdown_ref, gfin_ref,
                     y_ref, hist_ref, hist_sc, act_sc, *, shift, final_norm):
    tm = h_ref.shape[1]
    hr = hist_sc.shape[0]
    d_ff = wdown_ref.shape[0]
    step = pl.program_id(1)

    @pl.when(step == 0)
    def _load_history():
        hist_sc[...] = hist0_ref[0]

    h = h_ref[0]
    hn = _rmsnorm(h, g_ref[...]).astype(BF16)
    for c in range(d_ff // MXU_COLS):
        halves = []
        for base in (c * MXU_COLS, d_ff + c * MXU_COLS):
            cols = slice(base, base + MXU_COLS)
            up = _dot(hn, wup_ref[:, cols])
            ext = jnp.concatenate([hist_sc[:, cols], up], axis=0)
            conv = cb_ref[:, cols]
            for j in range(CONV_WIDTH):
                lo = hr - (CONV_WIDTH - 1 - j) * shift
                conv = conv + ext[lo:lo + tm] * cw_ref[j:j + 1, cols]
            hist_sc[:, cols] = ext[tm:tm + hr]
            halves.append(conv)
        gate = halves[0]
        act = 0.5 * gate * (1.0 + lax.erf(gate * (2.0 ** -0.5))) * halves[1]
        act_sc[:, c * MXU_COLS:(c + 1) * MXU_COLS] = act.astype(BF16)
    y = h + _dot(act_sc[...], wdown_ref[...])
    if final_norm:
        y = _rmsnorm(y, gfin_ref[...])
    y_ref[0] = y

    @pl.when(step == pl.num_programs(1) - 1)
    def _store_history():
        hist_ref[0] = hist_sc[...]


def _conv_ffn(h, hist0, g, w_up, conv_w, conv_b, w_down, g_final, tm, shift, final_norm):
    b, t, d = h.shape
    hr, f = hist0.shape[1], hist0.shape[2]
    d_ff = w_down.shape[0]

    consts = (g.reshape(1, d), w_up, conv_w, conv_b.reshape(1, f), w_down, g_final.reshape(1, d))
    return pl.pallas_call(
        functools.partial(_conv_ffn_kernel, shift=shift, final_norm=final_norm),
        grid=(b, t // tm),
        in_specs=[pl.BlockSpec((1, tm, d), lambda bi, i: (bi, i, 0)),
                  pl.BlockSpec((1, hr, f), lambda bi, i: (bi, 0, 0))] + [_const_spec(a, 2) for a in consts],
        out_specs=[pl.BlockSpec((1, tm, d), lambda bi, i: (bi, i, 0)),
                   pl.BlockSpec((1, hr, f), lambda bi, i: (bi, 0, 0))],
        out_shape=[jax.ShapeDtypeStruct((b, t, d), F32), jax.ShapeDtypeStruct((b, hr, f), F32)],
        scratch_shapes=[pltpu.VMEM((hr, f), F32), pltpu.VMEM((tm, d_ff), BF16)],
        compiler_params=_params("parallel", "arbitrary"),
        name="conv_ffn",
    )(h, hist0, *consts)


def _bias_rows(bias, rows_per_head, heads_per_stack):
    rows = jnp.repeat(bias.astype(F32), rows_per_head).reshape(-1, heads_per_stack * rows_per_head, 1)
    return jnp.broadcast_to(rows, rows.shape[:2] + (LANES,))


def _block_diag(w):
    groups, c, _ = w.shape
    eye = jnp.eye(groups, dtype=w.dtype)
    return (eye[:, None, :, None] * w[:, :, None, :]).reshape(groups * c, groups * c)


def kernel(x_prompt, x_sample, cache_k, cache_v, cache_mem_k, cache_mem_v, state_pool, state_ffn, page_table, mem_prompt, g_mix, w_in, sb_bias, w_gate, b_gate, w_pool, pool_scale, w_br_sb, w_br_pool, w_br_mem, w_out, g_mem, w_mem_kv, g_ffn, w_up, conv_w, conv_b, w_down, g_final):
    nb, seq, d = x_prompt.shape
    db, dseq, _ = x_sample.shape
    depth = w_in.shape[0]
    sb_w = w_br_sb.shape[1]
    pool_w = w_br_pool.shape[1]
    mem_w = w_br_mem.shape[1]
    f = w_up.shape[2]
    sb_heads = sb_w // HEAD_DIM
    page = cache_k.shape[2]
    past_len = page_table.shape[1] * page
    sb_scale = HEAD_DIM ** -0.5
    mem_scale = HEAD_DIM ** -0.5
    assert page == KEY_BLOCK and seq % KEY_BLOCK == 0 and seq >= POOL_BUF and dseq >= CONV_WIDTH - 1

    tri = _suffix_sum_matrix()
    hp, hs = x_prompt.reshape(nb * seq, d), x_sample.reshape(db * dseq, d)
    outs = [[] for _ in range(10)]
    for l in range(depth):
        last = l == depth - 1
        w_in_l = w_in[l].astype(BF16)
        o_q, o_k, o_v, o_u, o_m = 0, sb_w, 2 * sb_w, 3 * sb_w, 3 * sb_w + pool_w
        wbd = _block_diag(w_pool[l]).astype(BF16)
        scale = pool_scale[l].reshape(1, pool_w)
        merge_w = (g_mix[l], w_gate[l].astype(BF16), b_gate[l], w_br_sb[l].astype(BF16),
                   w_br_pool[l].astype(BF16), w_br_mem[l].astype(BF16), w_out[l].astype(BF16))
        ffn_w = (g_ffn[l], w_up[l].astype(BF16), conv_w[l], conv_b[l], w_down[l].astype(BF16), g_final)

        mk, mv = _norm_proj(mem_prompt.reshape(-1, d), g_mem[l], w_mem_kv[l].astype(BF16),
                            [(0, mem_w, F32, 1.0), (mem_w, mem_w, F32, 1.0)], tm=512)
        mk, mv = mk.reshape(nb, -1, mem_w), mv.reshape(nb, -1, mem_w)
        q, k, v, kb, vb, u, qm = _norm_proj(
            hp, g_mix[l], w_in_l,
            [(o_q, sb_w, BF16, sb_scale), (o_k, sb_w, F32, 1.0), (o_v, sb_w, F32, 1.0),
             (o_k, sb_w, BF16, 1.0), (o_v, sb_w, BF16, 1.0), (o_u, pool_w, F32, 1.0),
             (o_m, mem_w, F32, mem_scale)], tm=512)
        u3 = u.reshape(nb, seq, pool_w)
        o_sb = _sb_prompt(q.reshape(nb, seq, sb_w), kb.reshape(nb, seq, sb_w), vb.reshape(nb, seq, sb_w),
                          _bias_rows(sb_bias[l], KEY_BLOCK, 2), tri)
        o_pool = _pool_prompt(u3, wbd, scale, tm=512)
        o_mem = _mem_attn(qm.reshape(nb, seq, mem_w), mk, mv, tm=256)
        h = _merge(hp, o_sb.reshape(-1, sb_w), o_pool.reshape(-1, pool_w), o_mem.reshape(-1, mem_w),
                   *merge_w, tm=512)
        hist_rows = 8
        y, hist = _conv_ffn(h.reshape(nb, seq, d), jnp.zeros((nb, hist_rows, f), F32), *ffn_w,
                            tm=512, shift=1, final_norm=last)
        hp = y.reshape(nb * seq, d)
        for dst, val in zip(outs[0:2] + outs[4:6] + [outs[6], outs[8]],
                            (k.reshape(nb, seq, sb_heads, HEAD_DIM), v.reshape(nb, seq, sb_heads, HEAD_DIM),
                             mk.reshape(nb, -1, mem_w // HEAD_DIM, HEAD_DIM),
                             mv.reshape(nb, -1, mem_w // HEAD_DIM, HEAD_DIM),
                             u3[:, seq - POOL_BUF:], hist[:, hist_rows - (CONV_WIDTH - 1):])):
            dst.append(val)

        q, k, v, u, qm = _norm_proj(
            hs, g_mix[l], w_in_l,
            [(o_q, sb_w, F32, sb_scale), (o_k, sb_w, F32, 1.0), (o_v, sb_w, F32, 1.0),
             (o_u, pool_w, F32, 1.0), (o_m, mem_w, F32, mem_scale)], tm=512)
        k3, v3 = k.reshape(db, dseq, sb_w), v.reshape(db, dseq, sb_w)
        o_sb = _sb_sample(q.reshape(db, dseq, sb_w), k3, v3,
                          cache_k[l].reshape(-1, page, sb_w), cache_v[l].reshape(-1, page, sb_w),
                          page_table, _bias_rows(sb_bias[l], dseq, sb_heads)[0], tri, pages_per_step=8)
        ext = jnp.concatenate([state_pool[l], u.reshape(db, dseq, pool_w)], axis=1)
        o_pool = _pool_sample(ext.transpose(1, 0, 2), wbd, scale, past_len).transpose(1, 0, 2)
        o_mem = _mem_attn(qm.reshape(db, dseq, mem_w), cache_mem_k[l].reshape(db, -1, mem_w),
                          cache_mem_v[l].reshape(db, -1, mem_w), tm=dseq)
        h = _merge(hs, o_sb.reshape(-1, sb_w), o_pool.reshape(-1, pool_w), o_mem.reshape(-1, mem_w),
                   *merge_w, tm=512)
        h_t = h.reshape(db, dseq, d).transpose(1, 0, 2).reshape(1, dseq * db, d)
        hist0 = state_ffn[l].transpose(1, 0, 2).reshape(1, (CONV_WIDTH - 1) * db, f)
        y, hist = _conv_ffn(h_t, hist0, *ffn_w, tm=dseq * db, shift=db, final_norm=last)
        hs = y.reshape(dseq, db, d).transpose(1, 0, 2).reshape(db * dseq, d)
        for dst, val in zip(outs[2:4] + [outs[7], outs[9]],
                            (k3.reshape(db, dseq, sb_heads, HEAD_DIM), v3.reshape(db, dseq, sb_heads, HEAD_DIM),
                             ext[:, ext.shape[1] - POOL_BUF:], hist.reshape(CONV_WIDTH - 1, db, f).transpose(1, 0, 2))):
            dst.append(val)

    k_p, v_p, k_s, v_s, mk_p, mv_p, pool_p, pool_s, ffn_p, ffn_s = (jnp.stack(o) for o in outs)
    return (hp.reshape(nb, seq, d), hs.reshape(db, dseq, d), k_p, v_p, k_s, v_s,
            mk_p, mv_p, pool_p, pool_s, ffn_p, ffn_s)
```

```python
import functools

import jax
import jax.numpy as jnp
from jax import lax
from jax.experimental import pallas as pl
from jax.experimental.pallas import tpu as pltpu

F32 = jnp.float32
BF16 = jnp.bfloat16

NORM_EPS = 1e-6
LANES = 128
MXU_COLS = 256
HEAD_DIM = 64
POOL_WINDOWS = (2, 4, 8, 16)
POOL_BUF = max(POOL_WINDOWS) - 1
CONV_WIDTH = 3
PAGE = 128
PROMPT_BLOCK = 256
VMEM_LIMIT = 56 * 1024 * 1024


def _params(*sem):
    return pltpu.CompilerParams(dimension_semantics=sem, vmem_limit_bytes=VMEM_LIMIT)


def _rmsnorm(x, g):
    ms = jnp.mean(x * x, axis=-1, keepdims=True)
    return x * lax.rsqrt(ms + NORM_EPS) * g


def _div_pow2(x, n):
    assert n & (n - 1) == 0
    return lax.shift_right_logical(x, n.bit_length() - 1)


def _head_of(lane):
    return _div_pow2(lane, HEAD_DIM)


def _const_spec(a, grid_rank):
    index = {1: lambda i: (0,) * a.ndim, 2: lambda i, j: (0,) * a.ndim}[grid_rank]
    return pl.BlockSpec(a.shape, index, pipeline_mode=pl.Buffered(1))


def _dot(a, b):
    return jnp.dot(a, b, preferred_element_type=F32)


def _dot_nt(a, b):
    return lax.dot_general(a, b, (((1,), (1,)), ((), ())), preferred_element_type=F32)


def _norm_proj_kernel(*refs, cols, t_rows, blocked):
    refs = list(refs)
    x_ref, g_ref = refs.pop(0), refs.pop(0)
    w_ref = refs.pop(0) if cols else None
    wt_ref = refs.pop(0) if t_rows else None
    xn = _rmsnorm(x_ref[0], g_ref[...]).astype(BF16)
    for off, width, scale in cols:
        o_ref = refs.pop(0)
        y = _dot(xn, w_ref[:, off:off + width])
        if scale != 1.0:
            y = y * scale
        o_ref[0] = y.astype(o_ref.dtype)
    if t_rows:
        yt = _dot_nt(wt_ref[...], xn)
        off = 0
        for rows in t_rows:
            refs.pop(0)[0] = yt[off:off + rows]
            if blocked:
                b_ref = refs.pop(0)
                for r in range(b_ref.shape[1]):
                    for c in range(b_ref.shape[2]):
                        b_ref[0, r, c] = yt[off + r * LANES:off + (r + 1) * LANES,
                                            c * PROMPT_BLOCK:(c + 1) * PROMPT_BLOCK].astype(b_ref.dtype)
            off += rows


def _norm_proj(x, g, w, cols, wt, t_rows, blocked, tm):
    b, t, d = x.shape
    kern_cols = tuple((off, width, scale) for off, width, _, scale in cols)
    args, in_specs = [x, g.reshape(1, d)], [pl.BlockSpec((1, tm, d), lambda bi, i: (bi, i, 0)),
                                           _const_spec(g.reshape(1, d), 2)]
    for a in ((w,) if cols else ()) + ((wt,) if t_rows else ()):
        args.append(a)
        in_specs.append(_const_spec(a, 2))
    out_specs = [pl.BlockSpec((1, tm, width), lambda bi, i: (bi, i, 0)) for _, width, _, _ in cols]
    out_shape = [jax.ShapeDtypeStruct((b, t, width), dt) for _, width, dt, _ in cols]
    for rows in t_rows:
        out_specs.append(pl.BlockSpec((1, rows, tm), lambda bi, i: (bi, 0, i)))
        out_shape.append(jax.ShapeDtypeStruct((b, rows, t), F32))
        if blocked:
            nr, nc = rows // LANES, tm // PROMPT_BLOCK
            out_specs.append(pl.BlockSpec((1, nr, nc, LANES, PROMPT_BLOCK), lambda bi, i: (bi, 0, i, 0, 0)))
            out_shape.append(jax.ShapeDtypeStruct((b, nr, t // PROMPT_BLOCK, LANES, PROMPT_BLOCK), BF16))
    return pl.pallas_call(
        functools.partial(_norm_proj_kernel, cols=kern_cols, t_rows=tuple(t_rows), blocked=blocked),
        grid=(b, t // tm),
        in_specs=in_specs, out_specs=out_specs, out_shape=out_shape,
        compiler_params=_params("parallel", "parallel"),
        name="norm_proj",
    )(*args)


def _sb_weights(z, carry, tri, mask):
    log_keep = jnp.minimum(-z, 0.0) - jnp.log1p(jnp.exp(-jnp.abs(z)))
    if mask is not None:
        log_keep = jnp.where(mask, log_keep, 0.0)
    suffix = _dot(log_keep.astype(BF16), tri)
    a = jnp.exp(z + suffix + carry)
    if mask is not None:
        a = jnp.where(mask, a, 0.0)
    return a.astype(BF16), carry + jnp.sum(log_keep, axis=-1, keepdims=True)


def _suffix_sum_matrix(n):
    r = lax.broadcasted_iota(jnp.int32, (n, n), 0)
    c = lax.broadcasted_iota(jnp.int32, (n, n), 1)
    return jnp.where(r >= c, 1.0, 0.0).astype(BF16)


def _sb_prompt_kernel(q_ref, kt_ref, vt_ref, bias_ref, tri_ref, o_ref):
    tq = q_ref.shape[1]
    qi = pl.program_id(2)
    q2 = q_ref[0]
    lane = lax.broadcasted_iota(jnp.int32, (tq, LANES), 1)
    zero = jnp.zeros_like(q2)
    qbd = jnp.concatenate([jnp.where(lane < HEAD_DIM, q2, zero),
                           jnp.where(lane >= HEAD_DIM, q2, zero)], axis=0)
    bias = bias_ref[0]
    tri = tri_ref[...]

    def block(kb, carry, mask):
        z = _dot(qbd, kt_ref[0, 0, kb]) + bias
        a, carry = _sb_weights(z, carry, tri, mask)
        return _dot_nt(a, vt_ref[0, 0, kb]), carry

    row = lax.broadcasted_iota(jnp.int32, (2 * tq, tq), 0)
    col = lax.broadcasted_iota(jnp.int32, (2 * tq, tq), 1)
    causal = col < jnp.where(row >= tq, row - tq, row)
    acc, carry = block(qi, jnp.zeros((2 * tq, 1), F32), causal)

    def earlier(i, state):
        acc, carry = state
        o, carry = block(qi - 1 - i, carry, None)
        return acc + o, carry

    acc, _ = lax.fori_loop(0, qi, earlier, (acc, carry))
    o_ref[0] = jnp.where(lane < HEAD_DIM, acc[:tq], acc[tq:]).astype(o_ref.dtype)


def _sb_prompt(q, kt, vt, bias_rows, tri):
    b, t, width = q.shape
    tq = PROMPT_BLOCK
    nkb = t // tq
    kv_spec = pl.BlockSpec((1, 1, nkb, LANES, tq), lambda bi, hp, qi: (bi, hp, 0, 0, 0))
    return pl.pallas_call(
        _sb_prompt_kernel,
        grid=(b, width // LANES, t // tq),
        in_specs=[pl.BlockSpec((1, tq, LANES), lambda bi, hp, qi: (bi, qi, hp)),
                  kv_spec, kv_spec,
                  pl.BlockSpec((1, 2 * tq, 1), lambda bi, hp, qi: (hp, 0, 0)),
                  pl.BlockSpec((tq, tq), lambda bi, hp, qi: (0, 0))],
        out_specs=pl.BlockSpec((1, tq, LANES), lambda bi, hp, qi: (bi, qi, hp)),
        out_shape=jax.ShapeDtypeStruct((b, t, width), BF16),
        compiler_params=_params("parallel", "parallel", "arbitrary"),
        name="sb_prompt",
    )(q, kt, vt, bias_rows, tri)


def _sb_sample_kernel(pt_ref, q_ref, kn_ref, vn_ref, bias_ref, tri_ref, *rest, pages_per_step):
    del pt_ref
    n = pages_per_step
    k_refs, v_refs = rest[:n], rest[n:2 * n]
    o_ref, qbd_sc, carry_sc, acc_sc = rest[2 * n:]
    tq, width = q_ref.shape[1], q_ref.shape[2]
    heads = width // HEAD_DIM
    m = heads * tq
    step = pl.program_id(1)
    bias = bias_ref[...]
    tri = tri_ref[...]

    @pl.when(step == 0)
    def _new_tokens():
        qt = jnp.concatenate([q_ref[0]] * heads, axis=0)
        row = lax.broadcasted_iota(jnp.int32, (m, width), 0)
        lane = lax.broadcasted_iota(jnp.int32, (m, width), 1)
        qbd = jnp.where(_div_pow2(row, tq) == _head_of(lane), qt, 0.0).astype(BF16)
        qbd_sc[...] = qbd
        pad = jnp.zeros((PAGE - tq, width), F32)
        kn = jnp.concatenate([kn_ref[0], pad], axis=0).astype(BF16)
        vn = jnp.concatenate([vn_ref[0], pad], axis=0).astype(BF16)
        r = lax.broadcasted_iota(jnp.int32, (m, PAGE), 0)
        c = lax.broadcasted_iota(jnp.int32, (m, PAGE), 1)
        earlier_token = c < r - tq * _div_pow2(r, tq)
        a, carry = _sb_weights(_dot_nt(qbd, kn) + bias, jnp.zeros((m, 1), F32), tri, earlier_token)
        acc_sc[...] = _dot(a, vn)
        carry_sc[...] = carry

    kt = jnp.concatenate([k_ref[0] for k_ref in k_refs], axis=1).astype(BF16)
    z_wide = _dot(qbd_sc[...], kt)
    z = jnp.concatenate([z_wide[:, j * PAGE:(j + 1) * PAGE] for j in range(n)], axis=0)
    z = z + jnp.concatenate([bias] * n, axis=0)
    log_keep = jnp.minimum(-z, 0.0) - jnp.log1p(jnp.exp(-jnp.abs(z)))
    suffix = _dot(log_keep.astype(BF16), tri)
    totals = jnp.sum(log_keep, axis=-1, keepdims=True)
    carry, carries = carry_sc[...], []
    for j in range(n):
        carries.append(carry)
        carry = carry + totals[j * m:(j + 1) * m]
    carry_sc[...] = carry
    a = jnp.exp(z + suffix + jnp.concatenate(carries, axis=0)).astype(BF16)
    o = None
    for j, v_ref in enumerate(v_refs):
        term = _dot_nt(a[j * m:(j + 1) * m], v_ref[0].astype(BF16))
        o = term if o is None else o + term
    acc_sc[...] += o

    @pl.when(step == pl.num_programs(1) - 1)
    def _finish():
        lane = lax.broadcasted_iota(jnp.int32, (tq, LANES), 1)
        pieces = []
        for j in range(width // LANES):
            lo = acc_sc[2 * j * tq:(2 * j + 1) * tq, j * LANES:(j + 1) * LANES]
            hi = acc_sc[(2 * j + 1) * tq:(2 * j + 2) * tq, j * LANES:(j + 1) * LANES]
            pieces.append(jnp.where(lane < HEAD_DIM, lo, hi))
        o_ref[0] = jnp.concatenate(pieces, axis=1).astype(o_ref.dtype)


def _sb_sample(q, k_new, v_new, cache_kt, cache_vt, page_table, bias_rows, tri, pages_per_step):
    b, tq, width = q.shape
    n_pages = page_table.shape[1]
    m = (width // HEAD_DIM) * tq
    steps = n_pages // pages_per_step

    def page_spec(j):
        def index(bi, s, pt):
            return (pt[bi * n_pages + n_pages - 1 - (s * pages_per_step + j)], 0, 0)
        return pl.BlockSpec((1, width, PAGE), index)

    tok = pl.BlockSpec((1, tq, width), lambda bi, s, pt: (bi, 0, 0))
    grid_spec = pltpu.PrefetchScalarGridSpec(
        num_scalar_prefetch=1,
        grid=(b, steps),
        in_specs=[tok, tok, tok,
                  pl.BlockSpec((m, 1), lambda bi, s, pt: (0, 0)),
                  pl.BlockSpec((PAGE, PAGE), lambda bi, s, pt: (0, 0))]
                 + [page_spec(j) for _ in range(2) for j in range(pages_per_step)],
        out_specs=tok,
        scratch_shapes=[pltpu.VMEM((m, width), BF16),
                        pltpu.VMEM((m, 1), F32),
                        pltpu.VMEM((m, width), F32)],
    )
    return pl.pallas_call(
        functools.partial(_sb_sample_kernel, pages_per_step=pages_per_step),
        grid_spec=grid_spec,
        out_shape=jax.ShapeDtypeStruct((b, tq, width), BF16),
        compiler_params=_params("parallel", "arbitrary"),
        name="sb_sample",
    )(page_table.reshape(-1), q, k_new, v_new, bias_rows, tri,
      *([cache_kt] * pages_per_step), *([cache_vt] * pages_per_step))


def _mem_attn_kernel(q_ref, kt_ref, vt_ref, o_ref):
    tm, width = q_ref.shape[1], q_ref.shape[2]
    heads = width // HEAD_DIM
    q = q_ref[0]
    lane = lax.broadcasted_iota(jnp.int32, (tm, width), 1)
    qbd = jnp.concatenate([jnp.where(_head_of(lane) == h, q, 0.0) for h in range(heads)],
                          axis=0).astype(BF16)
    s = _dot(qbd, kt_ref[0].astype(BF16))
    p = jnp.exp(s - jnp.max(s, axis=-1, keepdims=True))
    denom = jnp.sum(p, axis=-1, keepdims=True)
    o = _dot_nt(p.astype(BF16), vt_ref[0].astype(BF16)) / denom
    out = o[:tm]
    for h in range(1, heads):
        out = jnp.where(_head_of(lane) == h, o[h * tm:(h + 1) * tm], out)
    o_ref[0] = out.astype(o_ref.dtype)


def _mem_attn(q, kt, vt, tm):
    b, t, width = q.shape
    mem = kt.shape[2]
    return pl.pallas_call(
        _mem_attn_kernel,
        grid=(b, t // tm),
        in_specs=[pl.BlockSpec((1, tm, width), lambda bi, i: (bi, i, 0)),
                  pl.BlockSpec((1, width, mem), lambda bi, i: (bi, 0, 0)),
                  pl.BlockSpec((1, width, mem), lambda bi, i: (bi, 0, 0))],
        out_specs=pl.BlockSpec((1, tm, width), lambda bi, i: (bi, i, 0)),
        out_shape=jax.ShapeDtypeStruct((b, t, width), BF16),
        compiler_params=_params("parallel", "parallel"),
        name="mem_attn",
    )(q, kt, vt)


def _window_sums(ext, t, axis):
    def cut(x, lo, n):
        return lax.slice_in_dim(x, lo, lo + n, axis=axis)

    length = ext.shape[axis]
    sums, cur, span = [], ext, 1
    for w in POOL_WINDOWS:
        while span < w:
            n = cur.shape[axis] - span
            cur = cut(cur, span, n) + cut(cur, 0, n)
            span *= 2
        sums.append(cut(cur, length - t - w + 1, t))
    return sums


def _pool_finish(sums, u, pos, wbd_ref, scale_ref, o_ref):
    lane = lax.broadcasted_iota(jnp.int32, u.shape, 1)
    total, window = sums[0], jnp.full(u.shape, POOL_WINDOWS[0], jnp.int32)
    for g in range(1, len(POOL_WINDOWS)):
        sel = _head_of(lane) == g
        total = jnp.where(sel, sums[g], total)
        window = jnp.where(sel, POOL_WINDOWS[g], window)
    count = jnp.minimum(window, pos + 1).astype(F32)
    pooled = total / count - u
    o_ref[...] = (_dot(pooled.astype(BF16), wbd_ref[...]) * scale_ref[...]).astype(o_ref.dtype).reshape(o_ref.shape)


def _pool_prompt_kernel(u_ref, halo_ref, wbd_ref, scale_ref, o_ref):
    tm = u_ref.shape[1]
    i = pl.program_id(1)
    u = u_ref[0]
    halo = jnp.where(i > 0, halo_ref[0], 0.0)
    ext = jnp.concatenate([halo, u], axis=0)
    pos = i * tm + lax.broadcasted_iota(jnp.int32, u.shape, 0)
    _pool_finish(_window_sums(ext, tm, 0), u, pos, wbd_ref, scale_ref, o_ref)


def _pool_prompt(u, wbd, scale, tm):
    b, t, width = u.shape
    halo = 16
    return pl.pallas_call(
        _pool_prompt_kernel,
        grid=(b, t // tm),
        in_specs=[pl.BlockSpec((1, tm, width), lambda bi, i: (bi, i, 0)),
                  pl.BlockSpec((1, halo, width), lambda bi, i: (bi, jnp.maximum(i * (tm // halo) - 1, 0), 0)),
                  pl.BlockSpec((width, width), lambda bi, i: (0, 0)),
                  pl.BlockSpec((1, width), lambda bi, i: (0, 0))],
        out_specs=pl.BlockSpec((1, tm, width), lambda bi, i: (bi, i, 0)),
        out_shape=jax.ShapeDtypeStruct((b, t, width), BF16),
        compiler_params=_params("parallel", "parallel"),
        name="pool_prompt",
    )(u, u, wbd, scale)


def _pool_sample_kernel(ext_ref, wbd_ref, scale_ref, o_ref, *, pos0):
    t, b, width = o_ref.shape
    ext = ext_ref[...]
    sums = [s.reshape(t * b, width) for s in _window_sums(ext, t, 0)]
    u = ext[POOL_BUF:].reshape(t * b, width)
    pos = pos0 + lax.broadcasted_iota(jnp.int32, (t, b, width), 0).reshape(t * b, width)
    _pool_finish(sums, u, pos, wbd_ref, scale_ref, o_ref)


def _pool_sample(ext, wbd, scale, pos0):
    rows, b, width = ext.shape
    t = rows - POOL_BUF
    return pl.pallas_call(
        functools.partial(_pool_sample_kernel, pos0=pos0),
        out_shape=jax.ShapeDtypeStruct((t, b, width), BF16),
        compiler_params=pltpu.CompilerParams(vmem_limit_bytes=VMEM_LIMIT),
        name="pool_sample",
    )(ext, wbd, scale)


def _merge_kernel(x_ref, sb_ref, pool_ref, mem_ref, g_ref, wg_ref, bg_ref,
                  wsb_ref, wpool_ref, wmem_ref, wout_ref, h_ref):
    d = x_ref.shape[1]
    x = x_ref[...]
    xn = _rmsnorm(x, g_ref[...]).astype(BF16)
    merged = None
    for j, (o_ref, w_ref) in enumerate(((sb_ref, wsb_ref), (pool_ref, wpool_ref), (mem_ref, wmem_ref))):
        gate = jax.nn.sigmoid(_dot(xn, wg_ref[:, j * d:(j + 1) * d]) + bg_ref[:, j * d:(j + 1) * d])
        term = gate * _dot(o_ref[...], w_ref[...])
        merged = term if merged is None else merged + term
    h_ref[...] = x + _dot(merged.astype(BF16), wout_ref[...])


def _merge(x, o_sb, o_pool, o_mem, g, w_gate, b_gate, w_sb, w_pool, w_mem, w_out, tm):
    rows, d = x.shape

    def rows_spec(a):
        return pl.BlockSpec((tm, a.shape[1]), lambda i: (i, 0))

    g2, bg2 = g.reshape(1, d), b_gate.reshape(1, -1)
    consts = (g2, w_gate, bg2, w_sb, w_pool, w_mem, w_out)
    return pl.pallas_call(
        _merge_kernel,
        grid=(rows // tm,),
        in_specs=[rows_spec(a) for a in (x, o_sb, o_pool, o_mem)] + [_const_spec(a, 1) for a in consts],
        out_specs=pl.BlockSpec((tm, d), lambda i: (i, 0)),
        out_shape=jax.ShapeDtypeStruct((rows, d), F32),
        compiler_params=_params("parallel"),
        name="merge",
    )(x, o_sb, o_pool, o_mem, *consts)


def _conv_ffn_kernel(h_ref, hist0_ref, g_ref, wup_ref, cw_ref, cb_ref, wdown_ref, gfin_ref,
                     y_ref, hist_ref, hist_sc, act_sc, *, shift, final_norm):
    tm = h_ref.shape[1]
    hr = hist_sc.shape[0]
    d_ff = wdown_ref.shape[0]
    step = pl.program_id(1)

    @pl.when(step == 0)
    def _load_history():
        hist_sc[...] = hist0_ref[0]

    h = h_ref[0]
    hn = _rmsnorm(h, g_ref[...]).astype(BF16)
    for c in range(d_ff // MXU_COLS):
        halves = []
        for base in (c * MXU_COLS, d_ff + c * MXU_COLS):
            cols = slice(base, base + MXU_COLS)
            up = _dot(hn, wup_ref[:, cols])
            ext = jnp.concatenate([hist_sc[:, cols], up], axis=0)
            conv = cb_ref[:, cols]
            for j in range(CONV_WIDTH):
                lo = hr - (CONV_WIDTH - 1 - j) * shift
                conv = conv + ext[lo:lo + tm] * cw_ref[j:j + 1, cols]
            hist_sc[:, cols] = ext[tm:tm + hr]
            halves.append(conv)
        gate = halves[0]
        act = 0.5 * gate * (1.0 + lax.erf(gate * (2.0 ** -0.5))) * halves[1]
        act_sc[:, c * MXU_COLS:(c + 1) * MXU_COLS] = act.astype(BF16)
    y = h + _dot(act_sc[...], wdown_ref[...])
    if final_norm:
        y = _rmsnorm(y, gfin_ref[...])
    y_ref[0] = y

    @pl.when(step == pl.num_programs(1) - 1)
    def _store_history():
        hist_ref[0] = hist_sc[...]


def _conv_ffn(h, hist0, g, w_up, conv_w, conv_b, w_down, g_final, tm, shift, final_norm):
    b, t, d = h.shape
    hr, f = hist0.shape[1], hist0.shape[2]
    d_ff = w_down.shape[0]
    consts = (g.reshape(1, d), w_up, conv_w, conv_b.reshape(1, f), w_down, g_final.reshape(1, d))
    return pl.pallas_call(
        functools.partial(_conv_ffn_kernel, shift=shift, final_norm=final_norm),
        grid=(b, t // tm),
        in_specs=[pl.BlockSpec((1, tm, d), lambda bi, i: (bi, i, 0)),
                  pl.BlockSpec((1, hr, f), lambda bi, i: (bi, 0, 0))] + [_const_spec(a, 2) for a in consts],
        out_specs=[pl.BlockSpec((1, tm, d), lambda bi, i: (bi, i, 0)),
                   pl.BlockSpec((1, hr, f), lambda bi, i: (bi, 0, 0))],
        out_shape=[jax.ShapeDtypeStruct((b, t, d), F32), jax.ShapeDtypeStruct((b, hr, f), F32)],
        scratch_shapes=[pltpu.VMEM((hr, f), F32), pltpu.VMEM((tm, d_ff), BF16)],
        compiler_params=_params("parallel", "arbitrary"),
        name="conv_ffn",
    )(h, hist0, *consts)


def _bias_rows(bias, rows_per_head, heads_per_stack):
    return jnp.repeat(bias.astype(F32), rows_per_head).reshape(-1, heads_per_stack * rows_per_head, 1)


def _block_diag(w):
    groups, c, _ = w.shape
    eye = jnp.eye(groups, dtype=w.dtype)
    return (eye[:, None, :, None] * w[:, :, None, :]).reshape(groups * c, groups * c)


def _heads_last(xt, heads):
    lead, n = xt.shape[:-2], xt.shape[-1]
    xt = xt.reshape(*lead, heads, HEAD_DIM, n)
    return jnp.moveaxis(xt, -1, -3)


def _heads_first_t(x):
    b, n, heads, hd = x.shape
    return x.transpose(0, 2, 3, 1).reshape(b, heads * hd, n)


def kernel(x_prompt, x_sample, cache_k, cache_v, cache_mem_k, cache_mem_v, state_pool, state_ffn, page_table, mem_prompt, g_mix, w_in, sb_bias, w_gate, b_gate, w_pool, pool_scale, w_br_sb, w_br_pool, w_br_mem, w_out, g_mem, w_mem_kv, g_ffn, w_up, conv_w, conv_b, w_down, g_final):
    nb, seq, d = x_prompt.shape
    db, dseq, _ = x_sample.shape
    depth = w_in.shape[0]
    sb_w = w_br_sb.shape[1]
    pool_w = w_br_pool.shape[1]
    mem_w = w_br_mem.shape[1]
    f = w_up.shape[2]
    sb_heads = sb_w // HEAD_DIM
    mem_heads = mem_w // HEAD_DIM
    past_len = page_table.shape[1] * cache_k.shape[2]
    sb_scale = HEAD_DIM ** -0.5
    mem_scale = HEAD_DIM ** -0.5
    assert cache_k.shape[2] == PAGE and seq % (2 * PROMPT_BLOCK) == 0
    assert seq >= POOL_BUF and dseq >= CONV_WIDTH - 1

    hp, hs = x_prompt, x_sample
    outs = [[] for _ in range(10)]
    for l in range(depth):
        last = l == depth - 1
        w_in_l = w_in[l].astype(BF16)
        o_q, o_k, o_v, o_u, o_m = 0, sb_w, 2 * sb_w, 3 * sb_w, 3 * sb_w + pool_w
        w_kv_t = w_in_l[:, o_k:o_u].T
        wbd = _block_diag(w_pool[l]).astype(BF16)
        scale = pool_scale[l].reshape(1, pool_w)
        merge_w = (g_mix[l], w_gate[l].astype(BF16), b_gate[l], w_br_sb[l].astype(BF16),
                   w_br_pool[l].astype(BF16), w_br_mem[l].astype(BF16), w_out[l].astype(BF16))
        ffn_w = (g_ffn[l], w_up[l].astype(BF16), conv_w[l], conv_b[l], w_down[l].astype(BF16), g_final)

        mkt, mvt = _norm_proj(mem_prompt, g_mem[l], None, (), w_mem_kv[l].astype(BF16).T,
                              (mem_w, mem_w), False, tm=mem_prompt.shape[1])
        q, u, qm, kt, ktb, vt, vtb = _norm_proj(
            hp, g_mix[l], w_in_l,
            ((o_q, sb_w, BF16, sb_scale), (o_u, pool_w, F32, 1.0), (o_m, mem_w, F32, mem_scale)),
            w_kv_t, (sb_w, sb_w), True, tm=512)
        o_sb = _sb_prompt(q, ktb, vtb, _bias_rows(sb_bias[l], PROMPT_BLOCK, 2),
                          _suffix_sum_matrix(PROMPT_BLOCK))
        o_pool = _pool_prompt(u, wbd, scale, tm=512)
        o_mem = _mem_attn(qm, mkt, mvt, tm=256)
        h = _merge(hp.reshape(nb * seq, d), o_sb.reshape(-1, sb_w), o_pool.reshape(-1, pool_w),
                   o_mem.reshape(-1, mem_w), *merge_w, tm=512)
        hist_rows = 8
        hp, hist = _conv_ffn(h.reshape(nb, seq, d), jnp.zeros((nb, hist_rows, f), F32), *ffn_w,
                             tm=512, shift=1, final_norm=last)
        for i, val in zip((0, 1, 4, 5, 6, 8),
                          (_heads_last(kt, sb_heads), _heads_last(vt, sb_heads),
                           _heads_last(mkt, mem_heads), _heads_last(mvt, mem_heads),
                           u[:, seq - POOL_BUF:], hist[:, hist_rows - (CONV_WIDTH - 1):])):
            outs[i].append(val)

        q, k, v, u, qm = (a.reshape(db, dseq, -1) for a in _norm_proj(
            hs.reshape(1, db * dseq, d), g_mix[l], w_in_l,
            ((o_q, sb_w, F32, sb_scale), (o_k, sb_w, F32, 1.0), (o_v, sb_w, F32, 1.0),
             (o_u, pool_w, F32, 1.0), (o_m, mem_w, F32, mem_scale)), None, (), False, tm=512))
        o_sb = _sb_sample(q, k, v, _heads_first_t(cache_k[l]), _heads_first_t(cache_v[l]), page_table,
                          _bias_rows(sb_bias[l], dseq, sb_heads)[0], _suffix_sum_matrix(PAGE), pages_per_step=8)
        ext = jnp.concatenate([state_pool[l].transpose(1, 0, 2), u.transpose(1, 0, 2)], axis=0)
        o_pool = _pool_sample(ext, wbd, scale, past_len).transpose(1, 0, 2)
        o_mem = _mem_attn(qm, _heads_first_t(cache_mem_k[l]), _heads_first_t(cache_mem_v[l]), tm=dseq)
        h = _merge(hs.reshape(db * dseq, d), o_sb.reshape(-1, sb_w), o_pool.reshape(-1, pool_w),
                   o_mem.reshape(-1, mem_w), *merge_w, tm=512)
        h_t = h.reshape(db, dseq, d).transpose(1, 0, 2).reshape(1, dseq * db, d)
        hist0 = state_ffn[l].transpose(1, 0, 2).reshape(1, (CONV_WIDTH - 1) * db, f)
        y, hist = _conv_ffn(h_t, hist0, *ffn_w, tm=dseq * db, shift=db, final_norm=last)
        hs = y.reshape(dseq, db, d).transpose(1, 0, 2)
        for i, val in zip((2, 3, 7, 9),
                          (k.reshape(db, dseq, sb_heads, HEAD_DIM), v.reshape(db, dseq, sb_heads, HEAD_DIM),
                           ext[ext.shape[0] - POOL_BUF:].transpose(1, 0, 2),
                           hist.reshape(CONV_WIDTH - 1, db, f).transpose(1, 0, 2))):
            outs[i].append(val)

    k_p, v_p, k_s, v_s, mk_p, mv_p, pool_p, pool_s, ffn_p, ffn_s = (jnp.stack(o) for o in outs)
    return (hp, hs, k_p, v_p, k_s, v_s, mk_p, mv_p, pool_p, pool_s, ffn_p, ffn_s)
```

```python
import functools

import jax
import jax.numpy as jnp
from jax import lax
from jax.experimental import pallas as pl
from jax.experimental.pallas import tpu as pltpu

F32 = jnp.float32
BF16 = jnp.bfloat16

NORM_EPS = 1e-6
LANES = 128
MXU_COLS = 256
HEAD_DIM = 64
POOL_WINDOWS = (2, 4, 8, 16)
POOL_BUF = max(POOL_WINDOWS) - 1
CONV_WIDTH = 3
PAGE = 128
PROMPT_BLOCK = 256
VMEM_LIMIT = 56 * 1024 * 1024


def _params(*sem):
    return pltpu.CompilerParams(dimension_semantics=sem, vmem_limit_bytes=VMEM_LIMIT)


def _rmsnorm(x, g):
    ms = jnp.mean(x * x, axis=-1, keepdims=True)
    return x * lax.rsqrt(ms + NORM_EPS) * g


def _div_pow2(x, n):
    assert n & (n - 1) == 0
    return lax.shift_right_logical(x, n.bit_length() - 1)


def _head_of(lane):
    return _div_pow2(lane, HEAD_DIM)


def _const_spec(a, grid_rank):
    index = {1: lambda i: (0,) * a.ndim, 2: lambda i, j: (0,) * a.ndim}[grid_rank]
    return pl.BlockSpec(a.shape, index, pipeline_mode=pl.Buffered(1))


def _dot(a, b):
    return jnp.dot(a, b, preferred_element_type=F32)


def _dot_nt(a, b):
    return lax.dot_general(a, b, (((1,), (1,)), ((), ())), preferred_element_type=F32)


def _norm_proj_kernel(*refs, cols, t_rows, blocked):
    refs = list(refs)
    x_ref, g_ref = refs.pop(0), refs.pop(0)
    w_ref = refs.pop(0) if cols else None
    wt_ref = refs.pop(0) if t_rows else None
    xn = _rmsnorm(x_ref[0], g_ref[...]).astype(BF16)
    for off, width, scale in cols:
        o_ref = refs.pop(0)
        y = _dot(xn, w_ref[:, off:off + width])
        if scale != 1.0:
            y = y * scale
        o_ref[0] = y.astype(o_ref.dtype)
    if t_rows:
        yt = _dot_nt(wt_ref[...], xn)
        off = 0
        for rows in t_rows:
            refs.pop(0)[0] = yt[off:off + rows]
            if blocked:
                b_ref = refs.pop(0)
                for r in range(b_ref.shape[1]):
                    for c in range(b_ref.shape[2]):
                        b_ref[0, r, c] = yt[off + r * LANES:off + (r + 1) * LANES,
                                            c * PROMPT_BLOCK:(c + 1) * PROMPT_BLOCK].astype(b_ref.dtype)
            off += rows


def _norm_proj(x, g, w, cols, wt, t_rows, blocked, tm):
    b, t, d = x.shape
    kern_cols = tuple((off, width, scale) for off, width, _, scale in cols)
    args, in_specs = [x, g.reshape(1, d)], [pl.BlockSpec((1, tm, d), lambda bi, i: (bi, i, 0)),
                                           _const_spec(g.reshape(1, d), 2)]
    for a in ((w,) if cols else ()) + ((wt,) if t_rows else ()):
        args.append(a)
        in_specs.append(_const_spec(a, 2))
    out_specs = [pl.BlockSpec((1, tm, width), lambda bi, i: (bi, i, 0)) for _, width, _, _ in cols]
    out_shape = [jax.ShapeDtypeStruct((b, t, width), dt) for _, width, dt, _ in cols]
    for rows in t_rows:
        out_specs.append(pl.BlockSpec((1, rows, tm), lambda bi, i: (bi, 0, i)))
        out_shape.append(jax.ShapeDtypeStruct((b, rows, t), F32))
        if blocked:
            nr, nc = rows // LANES, tm // PROMPT_BLOCK
            out_specs.append(pl.BlockSpec((1, nr, nc, LANES, PROMPT_BLOCK), lambda bi, i: (bi, 0, i, 0, 0)))
            out_shape.append(jax.ShapeDtypeStruct((b, nr, t // PROMPT_BLOCK, LANES, PROMPT_BLOCK), BF16))
    return pl.pallas_call(
        functools.partial(_norm_proj_kernel, cols=kern_cols, t_rows=tuple(t_rows), blocked=blocked),
        grid=(b, t // tm),
        in_specs=in_specs, out_specs=out_specs, out_shape=out_shape,
        compiler_params=_params("parallel", "parallel"),
        name="norm_proj",
    )(*args)


def _sb_weights(z, carry, tri, mask):
    drop = _softplus(z)
    if mask is not None:
        drop = jnp.where(mask, drop, 0.0)
    suffix = _dot(drop.astype(BF16), tri)
    a = jnp.exp(z - suffix - carry)
    if mask is not None:
        a = jnp.where(mask, a, 0.0)
    return a.astype(BF16), carry + jnp.sum(drop, axis=-1, keepdims=True)


def _softplus(z):
    return jnp.maximum(z, 0.0) + jnp.log(1.0 + jnp.exp(-jnp.abs(z)))


def _suffix_sum_matrix(n):
    r = lax.broadcasted_iota(jnp.int32, (n, n), 0)
    c = lax.broadcasted_iota(jnp.int32, (n, n), 1)
    return jnp.where(r >= c, 1.0, 0.0).astype(BF16)


def _sb_prompt_kernel(q_ref, kt_ref, vt_ref, bias_ref, tri_ref, o_ref):
    tq = q_ref.shape[1]
    pairs = q_ref.shape[2] // LANES
    qi = pl.program_id(2)
    lane = lax.broadcasted_iota(jnp.int32, (tq, LANES), 1)
    qbds = []
    for p in range(pairs):
        q2 = q_ref[0, :, p * LANES:(p + 1) * LANES]
        zero = jnp.zeros_like(q2)
        qbds.append(jnp.concatenate([jnp.where(lane < HEAD_DIM, q2, zero),
                                     jnp.where(lane >= HEAD_DIM, q2, zero)], axis=0))
    tri = tri_ref[...]

    def block(kb, state, mask):
        new_state = []
        for p, (acc, carry) in enumerate(state):
            z = _dot(qbds[p], kt_ref[0, p, kb]) + bias_ref[p]
            a, carry = _sb_weights(z, carry, tri, mask)
            o = _dot_nt(a, vt_ref[0, p, kb])
            new_state.append((o if acc is None else acc + o, carry))
        return tuple(new_state)

    row = lax.broadcasted_iota(jnp.int32, (2 * tq, tq), 0)
    col = lax.broadcasted_iota(jnp.int32, (2 * tq, tq), 1)
    causal = col < jnp.where(row >= tq, row - tq, row)
    state = block(qi, ((None, jnp.zeros((2 * tq, 1), F32)),) * pairs, causal)
    state = lax.fori_loop(0, qi, lambda i, st: block(qi - 1 - i, st, None), state)
    for p, (acc, _) in enumerate(state):
        o_ref[0, :, p * LANES:(p + 1) * LANES] = jnp.where(lane < HEAD_DIM, acc[:tq], acc[tq:]).astype(o_ref.dtype)


def _sb_prompt(q, kt, vt, bias_rows, tri, pairs_per_step):
    b, t, width = q.shape
    tq = PROMPT_BLOCK
    nkb = t // tq
    pp = pairs_per_step
    kv_spec = pl.BlockSpec((1, pp, nkb, LANES, tq), lambda bi, hp, qi: (bi, hp, 0, 0, 0))
    return pl.pallas_call(
        _sb_prompt_kernel,
        grid=(b, width // (pp * LANES), t // tq),
        in_specs=[pl.BlockSpec((1, tq, pp * LANES), lambda bi, hp, qi: (bi, qi, hp)),
                  kv_spec, kv_spec,
                  pl.BlockSpec((pp, 2 * tq, 1), lambda bi, hp, qi: (hp, 0, 0)),
                  pl.BlockSpec((tq, tq), lambda bi, hp, qi: (0, 0))],
        out_specs=pl.BlockSpec((1, tq, pp * LANES), lambda bi, hp, qi: (bi, qi, hp)),
        out_shape=jax.ShapeDtypeStruct((b, t, width), BF16),
        compiler_params=_params("parallel", "parallel", "arbitrary"),
        name="sb_prompt",
    )(q, kt, vt, bias_rows, tri)


def _sb_sample_kernel(pt_ref, q_ref, kn_ref, vn_ref, bias_ref, tri_ref, ck_hbm, cv_hbm, o_ref,
                      kbuf, vbuf, sems, qbd_sc, carry_sc, acc_new_sc, acc_t_sc, *, pages_per_step, n_pages):
    n = pages_per_step
    tq, width = q_ref.shape[1], q_ref.shape[2]
    heads = width // HEAD_DIM
    m = heads * tq
    bi, step = pl.program_id(0), pl.program_id(1)
    steps = pl.num_programs(1)
    g = bi * steps + step
    slot = lax.rem(g, 2)
    bias = bias_ref[...]
    tri = tri_ref[...]

    def page_copies(b_idx, s_idx, buf):
        copies = []
        for j in range(n):
            page = pt_ref[b_idx * n_pages + n_pages - 1 - (s_idx * n + j)]
            copies.append(pltpu.make_async_copy(ck_hbm.at[page], kbuf.at[buf, j], sems.at[0, buf]))
            copies.append(pltpu.make_async_copy(cv_hbm.at[page], vbuf.at[buf, j], sems.at[1, buf]))
        return copies

    @pl.when(g == 0)
    def _first_fetch():
        for c in page_copies(bi, step, slot):
            c.start()

    @pl.when(g + 1 < pl.num_programs(0) * steps)
    def _prefetch_next():
        wraps = step + 1 == steps
        for c in page_copies(jnp.where(wraps, bi + 1, bi), jnp.where(wraps, 0, step + 1), 1 - slot):
            c.start()

    @pl.when(step == 0)
    def _new_tokens():
        qt = jnp.concatenate([q_ref[0]] * heads, axis=0)
        row = lax.broadcasted_iota(jnp.int32, (m, width), 0)
        lane = lax.broadcasted_iota(jnp.int32, (m, width), 1)
        qbd = jnp.where(_div_pow2(row, tq) == _head_of(lane), qt, 0.0).astype(BF16)
        qbd_sc[...] = qbd
        pad = jnp.zeros((PAGE - tq, width), F32)
        kn = jnp.concatenate([kn_ref[0], pad], axis=0).astype(BF16)
        vn = jnp.concatenate([vn_ref[0], pad], axis=0).astype(BF16)
        r = lax.broadcasted_iota(jnp.int32, (m, PAGE), 0)
        c = lax.broadcasted_iota(jnp.int32, (m, PAGE), 1)
        earlier_token = c < r - tq * _div_pow2(r, tq)
        a, carry = _sb_weights(_dot_nt(qbd, kn) + bias, jnp.zeros((m, 1), F32), tri, earlier_token)
        acc_new_sc[...] = _dot(a, vn)
        acc_t_sc[...] = jnp.zeros_like(acc_t_sc)
        carry_sc[...] = carry

    for c in page_copies(bi, step, slot):
        c.wait()

    kt = jnp.concatenate([kbuf[slot, j] for j in range(n)], axis=1).astype(BF16)
    z = _dot(qbd_sc[...], kt) + bias
    drop = _softplus(z)
    stacked = jnp.concatenate([drop[:, j * PAGE:(j + 1) * PAGE] for j in range(n)], axis=0)
    suffix = _dot(stacked.astype(BF16), tri)
    totals = jnp.sum(stacked, axis=-1, keepdims=True)
    carry, weights = carry_sc[...], []
    for j in range(n):
        rows = slice(j * m, (j + 1) * m)
        weights.append(jnp.exp(z[:, j * PAGE:(j + 1) * PAGE] - suffix[rows] - carry))
        carry = carry + totals[rows]
    carry_sc[...] = carry
    a = jnp.concatenate(weights, axis=1)
    a = jnp.concatenate([a, jnp.zeros((LANES - m, n * PAGE), F32)], axis=0)
    vt = jnp.concatenate([vbuf[slot, j] for j in range(n)], axis=1).astype(BF16)
    acc_t_sc[...] += _dot(vt, a.T.astype(BF16))

    @pl.when(step == steps - 1)
    def _finish():
        acc = acc_new_sc[...] + acc_t_sc[...].T[:m]
        lane = lax.broadcasted_iota(jnp.int32, (tq, LANES), 1)
        pieces = []
        for j in range(width // LANES):
            lo = acc[2 * j * tq:(2 * j + 1) * tq, j * LANES:(j + 1) * LANES]
            hi = acc[(2 * j + 1) * tq:(2 * j + 2) * tq, j * LANES:(j + 1) * LANES]
            pieces.append(jnp.where(lane < HEAD_DIM, lo, hi))
        o_ref[0] = jnp.concatenate(pieces, axis=1).astype(o_ref.dtype)


def _sb_sample(q, k_new, v_new, cache_kt, cache_vt, page_table, bias_rows, tri, pages_per_step):
    b, tq, width = q.shape
    n_pages = page_table.shape[1]
    m = (width // HEAD_DIM) * tq
    assert n_pages % pages_per_step == 0 and m <= LANES
    tok = pl.BlockSpec((1, tq, width), lambda bi, s, pt: (bi, 0, 0))
    page_buffers = pltpu.VMEM((2, pages_per_step, width, PAGE), F32)
    grid_spec = pltpu.PrefetchScalarGridSpec(
        num_scalar_prefetch=1,
        grid=(b, n_pages // pages_per_step),
        in_specs=[tok, tok, tok,
                  pl.BlockSpec((m, 1), lambda bi, s, pt: (0, 0)),
                  pl.BlockSpec((PAGE, PAGE), lambda bi, s, pt: (0, 0)),
                  pl.BlockSpec(memory_space=pl.ANY),
                  pl.BlockSpec(memory_space=pl.ANY)],
        out_specs=tok,
        scratch_shapes=[page_buffers, page_buffers,
                        pltpu.SemaphoreType.DMA((2, 2)),
                        pltpu.VMEM((m, width), BF16),
                        pltpu.VMEM((m, 1), F32),
                        pltpu.VMEM((m, width), F32),
                        pltpu.VMEM((width, LANES), F32)],
    )
    return pl.pallas_call(
        functools.partial(_sb_sample_kernel, pages_per_step=pages_per_step, n_pages=n_pages),
        grid_spec=grid_spec,
        out_shape=jax.ShapeDtypeStruct((b, tq, width), BF16),
        compiler_params=_params("arbitrary", "arbitrary"),
        name="sb_sample",
    )(page_table.reshape(-1), q, k_new, v_new, bias_rows, tri, cache_kt, cache_vt)


def _mem_attn_kernel(q_ref, kt_ref, vt_ref, o_ref):
    nb, tm, width = q_ref.shape
    heads = width // HEAD_DIM
    lane = lax.broadcasted_iota(jnp.int32, (tm, width), 1)
    for i in range(nb):
        q = q_ref[i]
        qbd = jnp.concatenate([jnp.where(_head_of(lane) == h, q, 0.0) for h in range(heads)],
                              axis=0).astype(BF16)
        s = _dot(qbd, kt_ref[i].astype(BF16))
        p = jnp.exp(s - jnp.max(s, axis=-1, keepdims=True))
        denom = jnp.sum(p, axis=-1, keepdims=True)
        o = _dot_nt(p.astype(BF16), vt_ref[i].astype(BF16)) / denom
        out = o[:tm]
        for h in range(1, heads):
            out = jnp.where(_head_of(lane) == h, o[h * tm:(h + 1) * tm], out)
        o_ref[i] = out.astype(o_ref.dtype)


def _mem_attn(q, kt, vt, tm, nb):
    b, t, width = q.shape
    mem = kt.shape[2]
    return pl.pallas_call(
        _mem_attn_kernel,
        grid=(b // nb, t // tm),
        in_specs=[pl.BlockSpec((nb, tm, width), lambda bi, i: (bi, i, 0)),
                  pl.BlockSpec((nb, width, mem), lambda bi, i: (bi, 0, 0)),
                  pl.BlockSpec((nb, width, mem), lambda bi, i: (bi, 0, 0))],
        out_specs=pl.BlockSpec((nb, tm, width), lambda bi, i: (bi, i, 0)),
        out_shape=jax.ShapeDtypeStruct((b, t, width), BF16),
        compiler_params=_params("parallel", "parallel"),
        name="mem_attn",
    )(q, kt, vt)


def _window_sums(ext, t, axis):
    def cut(x, lo, n):
        return lax.slice_in_dim(x, lo, lo + n, axis=axis)

    length = ext.shape[axis]
    sums, cur, span = [], ext, 1
    for w in POOL_WINDOWS:
        while span < w:
            n = cur.shape[axis] - span
            cur = cut(cur, span, n) + cut(cur, 0, n)
            span *= 2
        sums.append(cut(cur, length - t - w + 1, t))
    return sums


def _pool_finish(sums, u, pos, wbd_ref, scale_ref, o_ref):
    lane = lax.broadcasted_iota(jnp.int32, u.shape, 1)
    total, window = sums[0], jnp.full(u.shape, POOL_WINDOWS[0], jnp.int32)
    for g in range(1, len(POOL_WINDOWS)):
        sel = _head_of(lane) == g
        total = jnp.where(sel, sums[g], total)
        window = jnp.where(sel, POOL_WINDOWS[g], window)
    count = jnp.minimum(window, pos + 1).astype(F32)
    pooled = total / count - u
    o_ref[...] = (_dot(pooled.astype(BF16), wbd_ref[...]) * scale_ref[...]).astype(o_ref.dtype).reshape(o_ref.shape)


def _pool_prompt_kernel(u_ref, halo_ref, wbd_ref, scale_ref, o_ref):
    tm = u_ref.shape[1]
    i = pl.program_id(1)
    u = u_ref[0]
    halo = jnp.where(i > 0, halo_ref[0], 0.0)
    ext = jnp.concatenate([halo, u], axis=0)
    pos = i * tm + lax.broadcasted_iota(jnp.int32, u.shape, 0)
    _pool_finish(_window_sums(ext, tm, 0), u, pos, wbd_ref, scale_ref, o_ref)


def _pool_prompt(u, wbd, scale, tm):
    b, t, width = u.shape
    halo = 16
    return pl.pallas_call(
        _pool_prompt_kernel,
        grid=(b, t // tm),
        in_specs=[pl.BlockSpec((1, tm, width), lambda bi, i: (bi, i, 0)),
                  pl.BlockSpec((1, halo, width), lambda bi, i: (bi, jnp.maximum(i * (tm // halo) - 1, 0), 0)),
                  pl.BlockSpec((width, width), lambda bi, i: (0, 0)),
                  pl.BlockSpec((1, width), lambda bi, i: (0, 0))],
        out_specs=pl.BlockSpec((1, tm, width), lambda bi, i: (bi, i, 0)),
        out_shape=jax.ShapeDtypeStruct((b, t, width), BF16),
        compiler_params=_params("parallel", "parallel"),
        name="pool_prompt",
    )(u, u, wbd, scale)


def _pool_sample_kernel(ext_ref, wbd_ref, scale_ref, o_ref, *, pos0):
    t, b, width = o_ref.shape
    ext = ext_ref[...]
    sums = [s.reshape(t * b, width) for s in _window_sums(ext, t, 0)]
    u = ext[POOL_BUF:].reshape(t * b, width)
    pos = pos0 + lax.broadcasted_iota(jnp.int32, (t, b, width), 0).reshape(t * b, width)
    _pool_finish(sums, u, pos, wbd_ref, scale_ref, o_ref)


def _pool_sample(ext, wbd, scale, pos0):
    rows, b, width = ext.shape
    t = rows - POOL_BUF
    return pl.pallas_call(
        functools.partial(_pool_sample_kernel, pos0=pos0),
        out_shape=jax.ShapeDtypeStruct((t, b, width), BF16),
        compiler_params=pltpu.CompilerParams(vmem_limit_bytes=VMEM_LIMIT),
        name="pool_sample",
    )(ext, wbd, scale)


def _merge_kernel(x_ref, sb_ref, pool_ref, mem_ref, g_ref, wg_ref, bg_ref,
                  wsb_ref, wpool_ref, wmem_ref, wout_ref, h_ref):
    d = x_ref.shape[1]
    x = x_ref[...]
    xn = _rmsnorm(x, g_ref[...]).astype(BF16)
    merged = None
    for j, (o_ref, w_ref) in enumerate(((sb_ref, wsb_ref), (pool_ref, wpool_ref), (mem_ref, wmem_ref))):
        gate = jax.nn.sigmoid(_dot(xn, wg_ref[:, j * d:(j + 1) * d]) + bg_ref[:, j * d:(j + 1) * d])
        term = gate * _dot(o_ref[...], w_ref[...])
        merged = term if merged is None else merged + term
    h_ref[...] = x + _dot(merged.astype(BF16), wout_ref[...])


def _merge(x, o_sb, o_pool, o_mem, g, w_gate, b_gate, w_sb, w_pool, w_mem, w_out, tm):
    rows, d = x.shape

    def rows_spec(a):
        return pl.BlockSpec((tm, a.shape[1]), lambda i: (i, 0))

    g2, bg2 = g.reshape(1, d), b_gate.reshape(1, -1)
    consts = (g2, w_gate, bg2, w_sb, w_pool, w_mem, w_out)
    return pl.pallas_call(
        _merge_kernel,
        grid=(rows // tm,),
        in_specs=[rows_spec(a) for a in (x, o_sb, o_pool, o_mem)] + [_const_spec(a, 1) for a in consts],
        out_specs=pl.BlockSpec((tm, d), lambda i: (i, 0)),
        out_shape=jax.ShapeDtypeStruct((rows, d), F32),
        compiler_params=_params("parallel"),
        name="merge",
    )(x, o_sb, o_pool, o_mem, *consts)


def _conv_ffn_kernel(h_ref, hist0_ref, g_ref, wup_ref, cw_ref, cb_ref, wdown_ref, gfin_ref,
                     y_ref, hist_ref, hist_sc, act_sc, *, shift, final_norm):
    tm = h_ref.shape[1]
    hr = hist_sc.shape[0]
    d_ff = wdown_ref.shape[0]
    step = pl.program_id(1)

    @pl.when(step == 0)
    def _load_history():
        hist_sc[...] = hist0_ref[0]

    h = h_ref[0]
    hn = _rmsnorm(h, g_ref[...]).astype(BF16)
    for c in range(d_ff // MXU_COLS):
        halves = []
        for base in (c * MXU_COLS, d_ff + c * MXU_COLS):
            cols = slice(base, base + MXU_COLS)
            up = _dot(hn, wup_ref[:, cols])
            ext = jnp.concatenate([hist_sc[:, cols], up], axis=0)
            conv = cb_ref[:, cols]
            for j in range(CONV_WIDTH):
                lo = hr - (CONV_WIDTH - 1 - j) * shift
                conv = conv + ext[lo:lo + tm] * cw_ref[j:j + 1, cols]
            hist_sc[:, cols] = ext[tm:tm + hr]
            halves.append(conv)
        gate = halves[0]
        act = 0.5 * gate * (1.0 + lax.erf(gate * (2.0 ** -0.5))) * halves[1]
        act_sc[:, c * MXU_COLS:(c + 1) * MXU_COLS] = act.astype(BF16)
    y = h + _dot(act_sc[...], wdown_ref[...])
    if final_norm:
        y = _rmsnorm(y, gfin_ref[...])
    y_ref[0] = y

    @pl.when(step == pl.num_programs(1) - 1)
    def _store_history():
        hist_ref[0] = hist_sc[...]


def _conv_ffn(h, hist0, g, w_up, conv_w, conv_b, w_down, g_final, tm, shift, final_norm):
    b, t, d = h.shape
    hr, f = hist0.shape[1], hist0.shape[2]
    d_ff = w_down.shape[0]
    consts = (g.reshape(1, d), w_up, conv_w, conv_b.reshape(1, f), w_down, g_final.reshape(1, d))
    return pl.pallas_call(
        functools.partial(_conv_ffn_kernel, shift=shift, final_norm=final_norm),
        grid=(b, t // tm),
        in_specs=[pl.BlockSpec((1, tm, d), lambda bi, i: (bi, i, 0)),
                  pl.BlockSpec((1, hr, f), lambda bi, i: (bi, 0, 0))] + [_const_spec(a, 2) for a in consts],
        out_specs=[pl.BlockSpec((1, tm, d), lambda bi, i: (bi, i, 0)),
                   pl.BlockSpec((1, hr, f), lambda bi, i: (bi, 0, 0))],
        out_shape=[jax.ShapeDtypeStruct((b, t, d), F32), jax.ShapeDtypeStruct((b, hr, f), F32)],
        scratch_shapes=[pltpu.VMEM((hr, f), F32), pltpu.VMEM((tm, d_ff), BF16)],
        compiler_params=_params("parallel", "arbitrary"),
        name="conv_ffn",
    )(h, hist0, *consts)


def _bias_rows(bias, rows_per_head, heads_per_stack):
    return jnp.repeat(bias.astype(F32), rows_per_head).reshape(-1, heads_per_stack * rows_per_head, 1)


def _block_diag(w):
    groups, c, _ = w.shape
    eye = jnp.eye(groups, dtype=w.dtype)
    return (eye[:, None, :, None] * w[:, :, None, :]).reshape(groups * c, groups * c)


def _heads_last(xt, heads):
    lead, n = xt.shape[:-2], xt.shape[-1]
    xt = xt.reshape(*lead, heads, HEAD_DIM, n)
    return jnp.moveaxis(xt, -1, -3)


def _heads_first_t(x):
    b, n, heads, hd = x.shape
    return x.transpose(0, 2, 3, 1).reshape(b, heads * hd, n)


def kernel(x_prompt, x_sample, cache_k, cache_v, cache_mem_k, cache_mem_v, state_pool, state_ffn, page_table, mem_prompt, g_mix, w_in, sb_bias, w_gate, b_gate, w_pool, pool_scale, w_br_sb, w_br_pool, w_br_mem, w_out, g_mem, w_mem_kv, g_ffn, w_up, conv_w, conv_b, w_down, g_final):
    nb, seq, d = x_prompt.shape
    db, dseq, _ = x_sample.shape
    depth = w_in.shape[0]
    sb_w = w_br_sb.shape[1]
    pool_w = w_br_pool.shape[1]
    mem_w = w_br_mem.shape[1]
    f = w_up.shape[2]
    sb_heads = sb_w // HEAD_DIM
    mem_heads = mem_w // HEAD_DIM
    past_len = page_table.shape[1] * cache_k.shape[2]
    sb_scale = HEAD_DIM ** -0.5
    mem_scale = HEAD_DIM ** -0.5
    assert cache_k.shape[2] == PAGE and seq % (2 * PROMPT_BLOCK) == 0
    assert seq >= POOL_BUF and dseq >= CONV_WIDTH - 1

    hp, hs = x_prompt, x_sample
    outs = [[] for _ in range(10)]
    for l in range(depth):
        last = l == depth - 1
        w_in_l = w_in[l].astype(BF16)
        o_q, o_k, o_v, o_u, o_m = 0, sb_w, 2 * sb_w, 3 * sb_w, 3 * sb_w + pool_w
        w_kv_t = w_in_l[:, o_k:o_u].T
        wbd = _block_diag(w_pool[l]).astype(BF16)
        scale = pool_scale[l].reshape(1, pool_w)
        merge_w = (g_mix[l], w_gate[l].astype(BF16), b_gate[l], w_br_sb[l].astype(BF16),
                   w_br_pool[l].astype(BF16), w_br_mem[l].astype(BF16), w_out[l].astype(BF16))
        ffn_w = (g_ffn[l], w_up[l].astype(BF16), conv_w[l], conv_b[l], w_down[l].astype(BF16), g_final)

        mkt, mvt = _norm_proj(mem_prompt, g_mem[l], None, (), w_mem_kv[l].astype(BF16).T,
                              (mem_w, mem_w), False, tm=mem_prompt.shape[1])
        q, u, qm, kt, ktb, vt, vtb = _norm_proj(
            hp, g_mix[l], w_in_l,
            ((o_q, sb_w, BF16, sb_scale), (o_u, pool_w, F32, 1.0), (o_m, mem_w, F32, mem_scale)),
            w_kv_t, (sb_w, sb_w), True, tm=512)
        o_sb = _sb_prompt(q, ktb, vtb, _bias_rows(sb_bias[l], PROMPT_BLOCK, 2),
                          _suffix_sum_matrix(PROMPT_BLOCK), pairs_per_step=2)
        o_pool = _pool_prompt(u, wbd, scale, tm=512)
        o_mem = _mem_attn(qm, mkt, mvt, tm=256, nb=1)
        h = _merge(hp.reshape(nb * seq, d), o_sb.reshape(-1, sb_w), o_pool.reshape(-1, pool_w),
                   o_mem.reshape(-1, mem_w), *merge_w, tm=512)
        hist_rows = 8
        hp, hist = _conv_ffn(h.reshape(nb, seq, d), jnp.zeros((nb, hist_rows, f), F32), *ffn_w,
                             tm=512, shift=1, final_norm=last)
        for i, val in zip((0, 1, 4, 5, 6, 8),
                          (_heads_last(kt, sb_heads), _heads_last(vt, sb_heads),
                           _heads_last(mkt, mem_heads), _heads_last(mvt, mem_heads),
                           u[:, seq - POOL_BUF:], hist[:, hist_rows - (CONV_WIDTH - 1):])):
            outs[i].append(val)

        q, k, v, u, qm = (a.reshape(db, dseq, -1) for a in _norm_proj(
            hs.reshape(1, db * dseq, d), g_mix[l], w_in_l,
            ((o_q, sb_w, F32, sb_scale), (o_k, sb_w, F32, 1.0), (o_v, sb_w, F32, 1.0),
             (o_u, pool_w, F32, 1.0), (o_m, mem_w, F32, mem_scale)), None, (), False, tm=512))
        o_sb = _sb_sample(q, k, v, _heads_first_t(cache_k[l]), _heads_first_t(cache_v[l]), page_table,
                          _bias_rows(sb_bias[l], dseq, sb_heads)[0], _suffix_sum_matrix(PAGE), pages_per_step=8)
        ext = jnp.concatenate([state_pool[l].transpose(1, 0, 2), u.transpose(1, 0, 2)], axis=0)
        o_pool = _pool_sample(ext, wbd, scale, past_len).transpose(1, 0, 2)
        o_mem = _mem_attn(qm, _heads_first_t(cache_mem_k[l]), _heads_first_t(cache_mem_v[l]), tm=dseq, nb=8)
        h = _merge(hs.reshape(db * dseq, d), o_sb.reshape(-1, sb_w), o_pool.reshape(-1, pool_w),
                   o_mem.reshape(-1, mem_w), *merge_w, tm=512)
        h_t = h.reshape(db, dseq, d).transpose(1, 0, 2).reshape(1, dseq * db, d)
        hist0 = state_ffn[l].transpose(1, 0, 2).reshape(1, (CONV_WIDTH - 1) * db, f)
        y, hist = _conv_ffn(h_t, hist0, *ffn_w, tm=dseq * db, shift=db, final_norm=last)
        hs = y.reshape(dseq, db, d).transpose(1, 0, 2)
        for i, val in zip((2, 3, 7, 9),
                          (k.reshape(db, dseq, sb_heads, HEAD_DIM), v.reshape(db, dseq, sb_heads, HEAD_DIM),
                           ext[ext.shape[0] - POOL_BUF:].transpose(1, 0, 2),
                           hist.reshape(CONV_WIDTH - 1, db, f).transpose(1, 0, 2))):
            outs[i].append(val)

    k_p, v_p, k_s, v_s, mk_p, mv_p, pool_p, pool_s, ffn_p, ffn_s = (jnp.stack(o) for o in outs)
    return (hp, hs, k_p, v_p, k_s, v_s, mk_p, mv_p, pool_p, pool_s, ffn_p, ffn_s)
```

```python
import functools

import jax
import jax.numpy as jnp
from jax import lax
from jax.experimental import pallas as pl
from jax.experimental.pallas import tpu as pltpu

F32 = jnp.float32
BF16 = jnp.bfloat16

NORM_EPS = 1e-6
LANES = 128
MXU_COLS = 256
HEAD_DIM = 64
POOL_WINDOWS = (2, 4, 8, 16)
POOL_BUF = max(POOL_WINDOWS) - 1
CONV_WIDTH = 3
PAGE = 128
PROMPT_BLOCK = 256
VMEM_LIMIT = 56 * 1024 * 1024


def _params(*sem):
    return pltpu.CompilerParams(dimension_semantics=sem, vmem_limit_bytes=VMEM_LIMIT)


def _rmsnorm(x, g):
    ms = jnp.mean(x * x, axis=-1, keepdims=True)
    return x * lax.rsqrt(ms + NORM_EPS) * g


def _div_pow2(x, n):
    assert n & (n - 1) == 0
    return lax.shift_right_logical(x, n.bit_length() - 1)


def _head_of(lane):
    return _div_pow2(lane, HEAD_DIM)


def _const_spec(a, grid_rank):
    index = {1: lambda i: (0,) * a.ndim, 2: lambda i, j: (0,) * a.ndim}[grid_rank]
    return pl.BlockSpec(a.shape, index, pipeline_mode=pl.Buffered(1))


def _dot(a, b):
    return jnp.dot(a, b, preferred_element_type=F32)


def _dot_nt(a, b):
    return lax.dot_general(a, b, (((1,), (1,)), ((), ())), preferred_element_type=F32)


def _norm_proj_kernel(*refs, cols, t_rows, blocked):
    refs = list(refs)
    x_ref, g_ref = refs.pop(0), refs.pop(0)
    w_ref = refs.pop(0) if cols else None
    wt_ref = refs.pop(0) if t_rows else None
    xn = _rmsnorm(x_ref[0], g_ref[...]).astype(BF16)
    for off, width, scale in cols:
        o_ref = refs.pop(0)
        y = _dot(xn, w_ref[:, off:off + width])
        if scale != 1.0:
            y = y * scale
        o_ref[0] = y.astype(o_ref.dtype)
    if t_rows:
        yt = _dot_nt(wt_ref[...], xn)
        off = 0
        for rows in t_rows:
            refs.pop(0)[0] = yt[off:off + rows]
            if blocked:
                b_ref = refs.pop(0)
                for r in range(b_ref.shape[1]):
                    for c in range(b_ref.shape[2]):
                        b_ref[0, r, c] = yt[off + r * LANES:off + (r + 1) * LANES,
                                            c * PROMPT_BLOCK:(c + 1) * PROMPT_BLOCK].astype(b_ref.dtype)
            off += rows


def _norm_proj(x, g, w, cols, wt, t_rows, blocked, tm):
    b, t, d = x.shape
    kern_cols = tuple((off, width, scale) for off, width, _, scale in cols)
    args, in_specs = [x, g.reshape(1, d)], [pl.BlockSpec((1, tm, d), lambda bi, i: (bi, i, 0)),
                                           _const_spec(g.reshape(1, d), 2)]
    for a in ((w,) if cols else ()) + ((wt,) if t_rows else ()):
        args.append(a)
        in_specs.append(_const_spec(a, 2))
    out_specs = [pl.BlockSpec((1, tm, width), lambda bi, i: (bi, i, 0)) for _, width, _, _ in cols]
    out_shape = [jax.ShapeDtypeStruct((b, t, width), dt) for _, width, dt, _ in cols]
    for rows in t_rows:
        out_specs.append(pl.BlockSpec((1, rows, tm), lambda bi, i: (bi, 0, i)))
        out_shape.append(jax.ShapeDtypeStruct((b, rows, t), F32))
        if blocked:
            nr, nc = rows // LANES, tm // PROMPT_BLOCK
            out_specs.append(pl.BlockSpec((1, nr, nc, LANES, PROMPT_BLOCK), lambda bi, i: (bi, 0, i, 0, 0)))
            out_shape.append(jax.ShapeDtypeStruct((b, nr, t // PROMPT_BLOCK, LANES, PROMPT_BLOCK), BF16))
    return pl.pallas_call(
        functools.partial(_norm_proj_kernel, cols=kern_cols, t_rows=tuple(t_rows), blocked=blocked),
        grid=(b, t // tm),
        in_specs=in_specs, out_specs=out_specs, out_shape=out_shape,
        compiler_params=_params("parallel", "parallel"),
        name="norm_proj",
    )(*args)


def _sb_weights(z, carry, tri, mask):
    drop = _softplus(z)
    if mask is not None:
        drop = jnp.where(mask, drop, 0.0)
    suffix = _dot(drop.astype(BF16), tri)
    a = jnp.exp(z - suffix - carry)
    if mask is not None:
        a = jnp.where(mask, a, 0.0)
    return a.astype(BF16), carry + jnp.sum(drop, axis=-1, keepdims=True)


def _softplus(z):
    return jnp.maximum(z, 0.0) + jnp.log(1.0 + jnp.exp(-jnp.abs(z)))


def _suffix_sum_matrix(n):
    r = lax.broadcasted_iota(jnp.int32, (n, n), 0)
    c = lax.broadcasted_iota(jnp.int32, (n, n), 1)
    return jnp.where(r >= c, 1.0, 0.0).astype(BF16)


def _sb_prompt_kernel(q_ref, kt_ref, vt_ref, bias_ref, tri_ref, o_ref):
    tq = q_ref.shape[1]
    pairs = q_ref.shape[2] // LANES
    qi = pl.program_id(2)
    lane = lax.broadcasted_iota(jnp.int32, (tq, LANES), 1)
    qbds = []
    for p in range(pairs):
        q2 = q_ref[0, :, p * LANES:(p + 1) * LANES]
        zero = jnp.zeros_like(q2)
        qbds.append(jnp.concatenate([jnp.where(lane < HEAD_DIM, q2, zero),
                                     jnp.where(lane >= HEAD_DIM, q2, zero)], axis=0))
    tri = tri_ref[...]

    def block(kb, state, mask):
        new_state = []
        for p, (acc, carry) in enumerate(state):
            z = _dot(qbds[p], kt_ref[0, p, kb]) + bias_ref[p]
            a, carry = _sb_weights(z, carry, tri, mask)
            o = _dot_nt(a, vt_ref[0, p, kb])
            new_state.append((o if acc is None else acc + o, carry))
        return tuple(new_state)

    row = lax.broadcasted_iota(jnp.int32, (2 * tq, tq), 0)
    col = lax.broadcasted_iota(jnp.int32, (2 * tq, tq), 1)
    causal = col < jnp.where(row >= tq, row - tq, row)
    state = block(qi, ((None, jnp.zeros((2 * tq, 1), F32)),) * pairs, causal)
    state = lax.fori_loop(0, qi, lambda i, st: block(qi - 1 - i, st, None), state)
    for p, (acc, _) in enumerate(state):
        o_ref[0, :, p * LANES:(p + 1) * LANES] = jnp.where(lane < HEAD_DIM, acc[:tq], acc[tq:]).astype(o_ref.dtype)


def _sb_prompt(q, kt, vt, bias_rows, tri, pairs_per_step):
    b, t, width = q.shape
    tq = PROMPT_BLOCK
    nkb = t // tq
    pp = pairs_per_step
    kv_spec = pl.BlockSpec((1, pp, nkb, LANES, tq), lambda bi, hp, qi: (bi, hp, 0, 0, 0))
    return pl.pallas_call(
        _sb_prompt_kernel,
        grid=(b, width // (pp * LANES), t // tq),
        in_specs=[pl.BlockSpec((1, tq, pp * LANES), lambda bi, hp, qi: (bi, qi, hp)),
                  kv_spec, kv_spec,
                  pl.BlockSpec((pp, 2 * tq, 1), lambda bi, hp, qi: (hp, 0, 0)),
                  pl.BlockSpec((tq, tq), lambda bi, hp, qi: (0, 0))],
        out_specs=pl.BlockSpec((1, tq, pp * LANES), lambda bi, hp, qi: (bi, qi, hp)),
        out_shape=jax.ShapeDtypeStruct((b, t, width), BF16),
        compiler_params=_params("parallel", "parallel", "arbitrary"),
        name="sb_prompt",
    )(q, kt, vt, bias_rows, tri)


def _sb_sample_kernel(pt_ref, q_ref, kn_ref, vn_ref, bias_ref, tri_ref, ck_hbm, cv_hbm, o_ref,
                      kbuf, vbuf, sems, qbd_sc, carry_sc, acc_new_sc, acc_t_sc, *, pages_per_step, n_pages):
    n = pages_per_step
    tq, width = q_ref.shape[1], q_ref.shape[2]
    heads = width // HEAD_DIM
    m = heads * tq
    bi, step = pl.program_id(0), pl.program_id(1)
    steps = pl.num_programs(1)
    g = bi * steps + step
    slot = lax.rem(g, 2)
    bias = bias_ref[...]
    tri = tri_ref[...]

    def page_copies(b_idx, s_idx, buf):
        copies = []
        for j in range(n):
            page = pt_ref[b_idx * n_pages + n_pages - 1 - (s_idx * n + j)]
            copies.append(pltpu.make_async_copy(ck_hbm.at[page], kbuf.at[buf, j], sems.at[0, buf]))
            copies.append(pltpu.make_async_copy(cv_hbm.at[page], vbuf.at[buf, j], sems.at[1, buf]))
        return copies

    def start_all(copies):
        for i, c in enumerate(copies):
            c.start(priority=i % 2)

    @pl.when(g == 0)
    def _first_fetch():
        start_all(page_copies(bi, step, slot))

    @pl.when(g + 1 < pl.num_programs(0) * steps)
    def _prefetch_next():
        wraps = step + 1 == steps
        start_all(page_copies(jnp.where(wraps, bi + 1, bi), jnp.where(wraps, 0, step + 1), 1 - slot))

    @pl.when(step == 0)
    def _new_tokens():
        qt = jnp.concatenate([q_ref[0]] * heads, axis=0)
        row = lax.broadcasted_iota(jnp.int32, (m, width), 0)
        lane = lax.broadcasted_iota(jnp.int32, (m, width), 1)
        qbd = jnp.where(_div_pow2(row, tq) == _head_of(lane), qt, 0.0).astype(BF16)
        qbd_sc[...] = qbd
        pad = jnp.zeros((PAGE - tq, width), F32)
        kn = jnp.concatenate([kn_ref[0], pad], axis=0).astype(BF16)
        vn = jnp.concatenate([vn_ref[0], pad], axis=0).astype(BF16)
        r = lax.broadcasted_iota(jnp.int32, (m, PAGE), 0)
        c = lax.broadcasted_iota(jnp.int32, (m, PAGE), 1)
        earlier_token = c < r - tq * _div_pow2(r, tq)
        a, carry = _sb_weights(_dot_nt(qbd, kn) + bias, jnp.zeros((m, 1), F32), tri, earlier_token)
        acc_new_sc[...] = _dot(a, vn)
        acc_t_sc[...] = jnp.zeros_like(acc_t_sc)
        carry_sc[...] = carry

    for c in page_copies(bi, step, slot):
        c.wait()

    kt = jnp.concatenate([kbuf[slot, j] for j in range(n)], axis=1).astype(BF16)
    z = _dot(qbd_sc[...], kt) + bias
    drop = _softplus(z)
    stacked = jnp.concatenate([drop[:, j * PAGE:(j + 1) * PAGE] for j in range(n)], axis=0)
    suffix = _dot(stacked.astype(BF16), tri)
    totals = jnp.sum(stacked, axis=-1, keepdims=True)
    carry, weights = carry_sc[...], []
    for j in range(n):
        rows = slice(j * m, (j + 1) * m)
        weights.append(jnp.exp(z[:, j * PAGE:(j + 1) * PAGE] - suffix[rows] - carry))
        carry = carry + totals[rows]
    carry_sc[...] = carry
    a = jnp.concatenate(weights, axis=1)
    a = jnp.concatenate([a, jnp.zeros((LANES - m, n * PAGE), F32)], axis=0)
    vt = jnp.concatenate([vbuf[slot, j] for j in range(n)], axis=1).astype(BF16)
    acc_t_sc[...] += _dot(vt, a.T.astype(BF16))

    @pl.when(step == steps - 1)
    def _finish():
        acc = acc_new_sc[...] + acc_t_sc[...].T[:m]
        lane = lax.broadcasted_iota(jnp.int32, (tq, LANES), 1)
        pieces = []
        for j in range(width // LANES):
            lo = acc[2 * j * tq:(2 * j + 1) * tq, j * LANES:(j + 1) * LANES]
            hi = acc[(2 * j + 1) * tq:(2 * j + 2) * tq, j * LANES:(j + 1) * LANES]
            pieces.append(jnp.where(lane < HEAD_DIM, lo, hi))
        o_ref[0] = jnp.concatenate(pieces, axis=1).astype(o_ref.dtype)


def _sb_sample(q, k_new, v_new, cache_kt, cache_vt, page_table, bias_rows, tri, pages_per_step):
    b, tq, width = q.shape
    n_pages = page_table.shape[1]
    m = (width // HEAD_DIM) * tq
    assert n_pages % pages_per_step == 0 and m <= LANES
    tok = pl.BlockSpec((1, tq, width), lambda bi, s, pt: (bi, 0, 0))
    page_buffers = pltpu.VMEM((2, pages_per_step, width, PAGE), F32)
    grid_spec = pltpu.PrefetchScalarGridSpec(
        num_scalar_prefetch=1,
        grid=(b, n_pages // pages_per_step),
        in_specs=[tok, tok, tok,
                  pl.BlockSpec((m, 1), lambda bi, s, pt: (0, 0)),
                  pl.BlockSpec((PAGE, PAGE), lambda bi, s, pt: (0, 0)),
                  pl.BlockSpec(memory_space=pl.ANY),
                  pl.BlockSpec(memory_space=pl.ANY)],
        out_specs=tok,
        scratch_shapes=[page_buffers, page_buffers,
                        pltpu.SemaphoreType.DMA((2, 2)),
                        pltpu.VMEM((m, width), BF16),
                        pltpu.VMEM((m, 1), F32),
                        pltpu.VMEM((m, width), F32),
                        pltpu.VMEM((width, LANES), F32)],
    )
    return pl.pallas_call(
        functools.partial(_sb_sample_kernel, pages_per_step=pages_per_step, n_pages=n_pages),
        grid_spec=grid_spec,
        out_shape=jax.ShapeDtypeStruct((b, tq, width), BF16),
        compiler_params=_params("arbitrary", "arbitrary"),
        name="sb_sample",
    )(page_table.reshape(-1), q, k_new, v_new, bias_rows, tri, cache_kt, cache_vt)


def _mem_attn_kernel(q_ref, kt_ref, vt_ref, o_ref):
    nb, tm, width = q_ref.shape
    heads = width // HEAD_DIM
    lane = lax.broadcasted_iota(jnp.int32, (tm, width), 1)
    for i in range(nb):
        q = q_ref[i]
        qbd = jnp.concatenate([jnp.where(_head_of(lane) == h, q, 0.0) for h in range(heads)],
                              axis=0).astype(BF16)
        s = _dot(qbd, kt_ref[i].astype(BF16))
        p = jnp.exp(s - jnp.max(s, axis=-1, keepdims=True))
        denom = jnp.sum(p, axis=-1, keepdims=True)
        o = _dot_nt(p.astype(BF16), vt_ref[i].astype(BF16)) / denom
        out = o[:tm]
        for h in range(1, heads):
            out = jnp.where(_head_of(lane) == h, o[h * tm:(h + 1) * tm], out)
        o_ref[i] = out.astype(o_ref.dtype)


def _mem_attn(q, kt, vt, tm, nb):
    b, t, width = q.shape
    mem = kt.shape[2]
    return pl.pallas_call(
        _mem_attn_kernel,
        grid=(b // nb, t // tm),
        in_specs=[pl.BlockSpec((nb, tm, width), lambda bi, i: (bi, i, 0)),
                  pl.BlockSpec((nb, width, mem), lambda bi, i: (bi, 0, 0)),
                  pl.BlockSpec((nb, width, mem), lambda bi, i: (bi, 0, 0))],
        out_specs=pl.BlockSpec((nb, tm, width), lambda bi, i: (bi, i, 0)),
        out_shape=jax.ShapeDtypeStruct((b, t, width), BF16),
        compiler_params=_params("parallel", "parallel"),
        name="mem_attn",
    )(q, kt, vt)


def _window_sums(ext, t, axis):
    def cut(x, lo, n):
        return lax.slice_in_dim(x, lo, lo + n, axis=axis)

    length = ext.shape[axis]
    sums, cur, span = [], ext, 1
    for w in POOL_WINDOWS:
        while span < w:
            n = cur.shape[axis] - span
            cur = cut(cur, span, n) + cut(cur, 0, n)
            span *= 2
        sums.append(cut(cur, length - t - w + 1, t))
    return sums


def _pool_finish(sums, u, pos, wbd_ref, scale_ref, o_ref):
    lane = lax.broadcasted_iota(jnp.int32, u.shape, 1)
    total, window = sums[0], jnp.full(u.shape, POOL_WINDOWS[0], jnp.int32)
    for g in range(1, len(POOL_WINDOWS)):
        sel = _head_of(lane) == g
        total = jnp.where(sel, sums[g], total)
        window = jnp.where(sel, POOL_WINDOWS[g], window)
    count = jnp.minimum(window, pos + 1).astype(F32)
    pooled = total / count - u
    o_ref[...] = (_dot(pooled.astype(BF16), wbd_ref[...]) * scale_ref[...]).astype(o_ref.dtype).reshape(o_ref.shape)


def _pool_prompt_kernel(u_ref, halo_ref, wbd_ref, scale_ref, o_ref):
    tm = u_ref.shape[1]
    i = pl.program_id(1)
    u = u_ref[0]
    halo = jnp.where(i > 0, halo_ref[0], 0.0)
    ext = jnp.concatenate([halo, u], axis=0)
    pos = i * tm + lax.broadcasted_iota(jnp.int32, u.shape, 0)
    _pool_finish(_window_sums(ext, tm, 0), u, pos, wbd_ref, scale_ref, o_ref)


def _pool_prompt(u, wbd, scale, tm):
    b, t, width = u.shape
    halo = 16
    return pl.pallas_call(
        _pool_prompt_kernel,
        grid=(b, t // tm),
        in_specs=[pl.BlockSpec((1, tm, width), lambda bi, i: (bi, i, 0)),
                  pl.BlockSpec((1, halo, width), lambda bi, i: (bi, jnp.maximum(i * (tm // halo) - 1, 0), 0)),
                  pl.BlockSpec((width, width), lambda bi, i: (0, 0)),
                  pl.BlockSpec((1, width), lambda bi, i: (0, 0))],
        out_specs=pl.BlockSpec((1, tm, width), lambda bi, i: (bi, i, 0)),
        out_shape=jax.ShapeDtypeStruct((b, t, width), BF16),
        compiler_params=_params("parallel", "parallel"),
        name="pool_prompt",
    )(u, u, wbd, scale)


def _pool_sample_kernel(ext_ref, wbd_ref, scale_ref, o_ref, *, pos0):
    t, b, width = o_ref.shape
    ext = ext_ref[...]
    sums = [s.reshape(t * b, width) for s in _window_sums(ext, t, 0)]
    u = ext[POOL_BUF:].reshape(t * b, width)
    pos = pos0 + lax.broadcasted_iota(jnp.int32, (t, b, width), 0).reshape(t * b, width)
    _pool_finish(sums, u, pos, wbd_ref, scale_ref, o_ref)


def _pool_sample(ext, wbd, scale, pos0):
    rows, b, width = ext.shape
    t = rows - POOL_BUF
    return pl.pallas_call(
        functools.partial(_pool_sample_kernel, pos0=pos0),
        out_shape=jax.ShapeDtypeStruct((t, b, width), BF16),
        compiler_params=pltpu.CompilerParams(vmem_limit_bytes=VMEM_LIMIT),
        name="pool_sample",
    )(ext, wbd, scale)


def _merge_kernel(x_ref, sb_ref, pool_ref, mem_ref, g_ref, wg_ref, bg_ref,
                  wsb_ref, wpool_ref, wmem_ref, wout_ref, h_ref):
    d = x_ref.shape[1]
    x = x_ref[...]
    xn = _rmsnorm(x, g_ref[...]).astype(BF16)
    merged = None
    for j, (o_ref, w_ref) in enumerate(((sb_ref, wsb_ref), (pool_ref, wpool_ref), (mem_ref, wmem_ref))):
        gate = jax.nn.sigmoid(_dot(xn, wg_ref[:, j * d:(j + 1) * d]) + bg_ref[:, j * d:(j + 1) * d])
        term = gate * _dot(o_ref[...], w_ref[...])
        merged = term if merged is None else merged + term
    h_ref[...] = x + _dot(merged.astype(BF16), wout_ref[...])


def _merge(x, o_sb, o_pool, o_mem, g, w_gate, b_gate, w_sb, w_pool, w_mem, w_out, tm):
    rows, d = x.shape

    def rows_spec(a):
        return pl.BlockSpec((tm, a.shape[1]), lambda i: (i, 0))

    g2, bg2 = g.reshape(1, d), b_gate.reshape(1, -1)
    consts = (g2, w_gate, bg2, w_sb, w_pool, w_mem, w_out)
    return pl.pallas_call(
        _merge_kernel,
        grid=(rows // tm,),
        in_specs=[rows_spec(a) for a in (x, o_sb, o_pool, o_mem)] + [_const_spec(a, 1) for a in consts],
        out_specs=pl.BlockSpec((tm, d), lambda i: (i, 0)),
        out_shape=jax.ShapeDtypeStruct((rows, d), F32),
        compiler_params=_params("parallel"),
        name="merge",
    )(x, o_sb, o_pool, o_mem, *consts)


def _conv_ffn_kernel(h_ref, hist0_ref, g_ref, wup_ref, cw_ref, cb_ref, wdown_ref, gfin_ref,
                     y_ref, hist_ref, hist_sc, act_sc, *, shift, final_norm):
    tm = h_ref.shape[1]
    hr = hist_sc.shape[0]
    d_ff = wdown_ref.shape[0]
    step = pl.program_id(1)

    @pl.when(step == 0)
    def _load_history():
        hist_sc[...] = hist0_ref[0]

    h = h_ref[0]
    hn = _rmsnorm(h, g_ref[...]).astype(BF16)
    for c in range(d_ff // MXU_COLS):
        halves = []
        for base in (c * MXU_COLS, d_ff + c * MXU_COLS):
            cols = slice(base, base + MXU_COLS)
            up = _dot(hn, wup_ref[:, cols])
            ext = jnp.concatenate([hist_sc[:, cols], up], axis=0)
            conv = cb_ref[:, cols]
            for j in range(CONV_WIDTH):
                lo = hr - (CONV_WIDTH - 1 - j) * shift
                conv = conv + ext[lo:lo + tm] * cw_ref[j:j + 1, cols]
            hist_sc[:, cols] = ext[tm:tm + hr]
            halves.append(conv)
        gate = halves[0]
        act = 0.5 * gate * (1.0 + lax.erf(gate * (2.0 ** -0.5))) * halves[1]
        act_sc[:, c * MXU_COLS:(c + 1) * MXU_COLS] = act.astype(BF16)
    y = h + _dot(act_sc[...], wdown_ref[...])
    if final_norm:
        y = _rmsnorm(y, gfin_ref[...])
    y_ref[0] = y

    @pl.when(step == pl.num_programs(1) - 1)
    def _store_history():
        hist_ref[0] = hist_sc[...]


def _conv_ffn(h, hist0, g, w_up, conv_w, conv_b, w_down, g_final, tm, shift, final_norm):
    b, t, d = h.shape
    hr, f = hist0.shape[1], hist0.shape[2]
    d_ff = w_down.shape[0]
    consts = (g.reshape(1, d), w_up, conv_w, conv_b.reshape(1, f), w_down, g_final.reshape(1, d))
    return pl.pallas_call(
        functools.partial(_conv_ffn_kernel, shift=shift, final_norm=final_norm),
        grid=(b, t // tm),
        in_specs=[pl.BlockSpec((1, tm, d), lambda bi, i: (bi, i, 0)),
                  pl.BlockSpec((1, hr, f), lambda bi, i: (bi, 0, 0))] + [_const_spec(a, 2) for a in consts],
        out_specs=[pl.BlockSpec((1, tm, d), lambda bi, i: (bi, i, 0)),
                   pl.BlockSpec((1, hr, f), lambda bi, i: (bi, 0, 0))],
        out_shape=[jax.ShapeDtypeStruct((b, t, d), F32), jax.ShapeDtypeStruct((b, hr, f), F32)],
        scratch_shapes=[pltpu.VMEM((hr, f), F32), pltpu.VMEM((tm, d_ff), BF16)],
        compiler_params=_params("parallel", "arbitrary"),
        name="conv_ffn",
    )(h, hist0, *consts)


def _bias_rows(bias, rows_per_head, heads_per_stack):
    return jnp.repeat(bias.astype(F32), rows_per_head).reshape(-1, heads_per_stack * rows_per_head, 1)


def _block_diag(w):
    groups, c, _ = w.shape
    eye = jnp.eye(groups, dtype=w.dtype)
    return (eye[:, None, :, None] * w[:, :, None, :]).reshape(groups * c, groups * c)


def _heads_last(xt, heads):
    lead, n = xt.shape[:-2], xt.shape[-1]
    xt = xt.reshape(*lead, heads, HEAD_DIM, n)
    return jnp.moveaxis(xt, -1, -3)


def _heads_first_t(x):
    b, n, heads, hd = x.shape
    return x.transpose(0, 2, 3, 1).reshape(b, heads * hd, n)


def kernel(x_prompt, x_sample, cache_k, cache_v, cache_mem_k, cache_mem_v, state_pool, state_ffn, page_table, mem_prompt, g_mix, w_in, sb_bias, w_gate, b_gate, w_pool, pool_scale, w_br_sb, w_br_pool, w_br_mem, w_out, g_mem, w_mem_kv, g_ffn, w_up, conv_w, conv_b, w_down, g_final):
    nb, seq, d = x_prompt.shape
    db, dseq, _ = x_sample.shape
    depth = w_in.shape[0]
    sb_w = w_br_sb.shape[1]
    pool_w = w_br_pool.shape[1]
    mem_w = w_br_mem.shape[1]
    f = w_up.shape[2]
    sb_heads = sb_w // HEAD_DIM
    mem_heads = mem_w // HEAD_DIM
    past_len = page_table.shape[1] * cache_k.shape[2]
    sb_scale = HEAD_DIM ** -0.5
    mem_scale = HEAD_DIM ** -0.5
    assert cache_k.shape[2] == PAGE and seq % (2 * PROMPT_BLOCK) == 0
    assert seq >= POOL_BUF and dseq >= CONV_WIDTH - 1

    hp, hs = x_prompt, x_sample
    outs = [[] for _ in range(10)]
    for l in range(depth):
        last = l == depth - 1
        w_in_l = w_in[l].astype(BF16)
        o_q, o_k, o_v, o_u, o_m = 0, sb_w, 2 * sb_w, 3 * sb_w, 3 * sb_w + pool_w
        w_kv_t = w_in_l[:, o_k:o_u].T
        wbd = _block_diag(w_pool[l]).astype(BF16)
        scale = pool_scale[l].reshape(1, pool_w)
        merge_w = (g_mix[l], w_gate[l].astype(BF16), b_gate[l], w_br_sb[l].astype(BF16),
                   w_br_pool[l].astype(BF16), w_br_mem[l].astype(BF16), w_out[l].astype(BF16))
        ffn_w = (g_ffn[l], w_up[l].astype(BF16), conv_w[l], conv_b[l], w_down[l].astype(BF16), g_final)

        mkt, mvt = _norm_proj(mem_prompt, g_mem[l], None, (), w_mem_kv[l].astype(BF16).T,
                              (mem_w, mem_w), False, tm=mem_prompt.shape[1])
        q, u, qm, kt, ktb, vt, vtb = _norm_proj(
            hp, g_mix[l], w_in_l,
            ((o_q, sb_w, BF16, sb_scale), (o_u, pool_w, F32, 1.0), (o_m, mem_w, F32, mem_scale)),
            w_kv_t, (sb_w, sb_w), True, tm=512)
        o_sb = _sb_prompt(q, ktb, vtb, _bias_rows(sb_bias[l], PROMPT_BLOCK, 2),
                          _suffix_sum_matrix(PROMPT_BLOCK), pairs_per_step=2)
        o_pool = _pool_prompt(u, wbd, scale, tm=512)
        o_mem = _mem_attn(qm, mkt, mvt, tm=256, nb=1)
        h = _merge(hp.reshape(nb * seq, d), o_sb.reshape(-1, sb_w), o_pool.reshape(-1, pool_w),
                   o_mem.reshape(-1, mem_w), *merge_w, tm=512)
        hist_rows = 8
        hp, hist = _conv_ffn(h.reshape(nb, seq, d), jnp.zeros((nb, hist_rows, f), F32), *ffn_w,
                             tm=512, shift=1, final_norm=last)
        for i, val in zip((0, 1, 4, 5, 6, 8),
                          (_heads_last(kt, sb_heads), _heads_last(vt, sb_heads),
                           _heads_last(mkt, mem_heads), _heads_last(mvt, mem_heads),
                           u[:, seq - POOL_BUF:], hist[:, hist_rows - (CONV_WIDTH - 1):])):
            outs[i].append(val)

        q, k, v, u, qm = (a.reshape(db, dseq, -1) for a in _norm_proj(
            hs.reshape(1, db * dseq, d), g_mix[l], w_in_l,
            ((o_q, sb_w, F32, sb_scale), (o_k, sb_w, F32, 1.0), (o_v, sb_w, F32, 1.0),
             (o_u, pool_w, F32, 1.0), (o_m, mem_w, F32, mem_scale)), None, (), False, tm=512))
        o_sb = _sb_sample(q, k, v, _heads_first_t(cache_k[l]), _heads_first_t(cache_v[l]), page_table,
                          _bias_rows(sb_bias[l], dseq, sb_heads)[0], _suffix_sum_matrix(PAGE), pages_per_step=16)
        ext = jnp.concatenate([state_pool[l].transpose(1, 0, 2), u.transpose(1, 0, 2)], axis=0)
        o_pool = _pool_sample(ext, wbd, scale, past_len).transpose(1, 0, 2)
        o_mem = _mem_attn(qm, _heads_first_t(cache_mem_k[l]), _heads_first_t(cache_mem_v[l]), tm=dseq, nb=8)
        h = _merge(hs.reshape(db * dseq, d), o_sb.reshape(-1, sb_w), o_pool.reshape(-1, pool_w),
                   o_mem.reshape(-1, mem_w), *merge_w, tm=512)
        h_t = h.reshape(db, dseq, d).transpose(1, 0, 2).reshape(1, dseq * db, d)
        hist0 = state_ffn[l].transpose(1, 0, 2).reshape(1, (CONV_WIDTH - 1) * db, f)
        y, hist = _conv_ffn(h_t, hist0, *ffn_w, tm=dseq * db, shift=db, final_norm=last)
        hs = y.reshape(dseq, db, d).transpose(1, 0, 2)
        for i, val in zip((2, 3, 7, 9),
                          (k.reshape(db, dseq, sb_heads, HEAD_DIM), v.reshape(db, dseq, sb_heads, HEAD_DIM),
                           ext[ext.shape[0] - POOL_BUF:].transpose(1, 0, 2),
                           hist.reshape(CONV_WIDTH - 1, db, f).transpose(1, 0, 2))):
            outs[i].append(val)

    k_p, v_p, k_s, v_s, mk_p, mv_p, pool_p, pool_s, ffn_p, ffn_s = (jnp.stack(o) for o in outs)
    return (hp, hs, k_p, v_p, k_s, v_s, mk_p, mv_p, pool_p, pool_s, ffn_p, ffn_s)
```

```python
import functools

import jax
import jax.numpy as jnp
import numpy as np
from jax import lax
from jax.experimental import pallas as pl
from jax.experimental.pallas import tpu as pltpu

F32 = jnp.float32
BF16 = jnp.bfloat16

NORM_EPS = 1e-6
LANES = 128
MXU_COLS = 256
HEAD_DIM = 64
POOL_WINDOWS = (2, 4, 8, 16)
POOL_BUF = max(POOL_WINDOWS) - 1
CONV_WIDTH = 3
PAGE = 128
PROMPT_BLOCK = 256
VMEM_LIMIT = 56 * 1024 * 1024


def _params(*sem):
    return pltpu.CompilerParams(dimension_semantics=sem, vmem_limit_bytes=VMEM_LIMIT)


def _rmsnorm(x, g):
    ms = jnp.mean(x * x, axis=-1, keepdims=True)
    return x * lax.rsqrt(ms + NORM_EPS) * g


def _div_pow2(x, n):
    assert n & (n - 1) == 0
    return lax.shift_right_logical(x, n.bit_length() - 1)


def _head_of(lane):
    return _div_pow2(lane, HEAD_DIM)


def _const_spec(a, grid_rank):
    index = {1: lambda i: (0,) * a.ndim, 2: lambda i, j: (0,) * a.ndim}[grid_rank]
    return pl.BlockSpec(a.shape, index, pipeline_mode=pl.Buffered(1))


def _dot(a, b):
    return jnp.dot(a, b, preferred_element_type=F32)


def _dot_nt(a, b):
    return lax.dot_general(a, b, (((1,), (1,)), ((), ())), preferred_element_type=F32)


def _norm_proj_kernel(*refs, cols, t_rows, blocked):
    refs = list(refs)
    x_ref, g_ref = refs.pop(0), refs.pop(0)
    w_ref = refs.pop(0) if cols else None
    wt_ref = refs.pop(0) if t_rows else None
    xn = _rmsnorm(x_ref[0], g_ref[...]).astype(BF16)
    for off, width, scale in cols:
        o_ref = refs.pop(0)
        y = _dot(xn, w_ref[:, off:off + width])
        if scale != 1.0:
            y = y * scale
        o_ref[0] = y.astype(o_ref.dtype)
    if t_rows:
        yt = _dot_nt(wt_ref[...], xn)
        off = 0
        for rows in t_rows:
            refs.pop(0)[0] = yt[off:off + rows]
            if blocked:
                b_ref = refs.pop(0)
                for r in range(b_ref.shape[1]):
                    for c in range(b_ref.shape[2]):
                        b_ref[0, r, c] = yt[off + r * LANES:off + (r + 1) * LANES,
                                            c * PROMPT_BLOCK:(c + 1) * PROMPT_BLOCK].astype(b_ref.dtype)
            off += rows


def _norm_proj(x, g, w, cols, wt, t_rows, blocked, tm):
    b, t, d = x.shape
    kern_cols = tuple((off, width, scale) for off, width, _, scale in cols)
    args, in_specs = [x, g.reshape(1, d)], [pl.BlockSpec((1, tm, d), lambda bi, i: (bi, i, 0)),
                                           _const_spec(g.reshape(1, d), 2)]
    for a in ((w,) if cols else ()) + ((wt,) if t_rows else ()):
        args.append(a)
        in_specs.append(_const_spec(a, 2))
    out_specs = [pl.BlockSpec((1, tm, width), lambda bi, i: (bi, i, 0)) for _, width, _, _ in cols]
    out_shape = [jax.ShapeDtypeStruct((b, t, width), dt) for _, width, dt, _ in cols]
    for rows in t_rows:
        out_specs.append(pl.BlockSpec((1, rows, tm), lambda bi, i: (bi, 0, i)))
        out_shape.append(jax.ShapeDtypeStruct((b, rows, t), F32))
        if blocked:
            nr, nc = rows // LANES, tm // PROMPT_BLOCK
            out_specs.append(pl.BlockSpec((1, nr, nc, LANES, PROMPT_BLOCK), lambda bi, i: (bi, 0, i, 0, 0)))
            out_shape.append(jax.ShapeDtypeStruct((b, nr, t // PROMPT_BLOCK, LANES, PROMPT_BLOCK), BF16))
    return pl.pallas_call(
        functools.partial(_norm_proj_kernel, cols=kern_cols, t_rows=tuple(t_rows), blocked=blocked),
        grid=(b, t // tm),
        in_specs=in_specs, out_specs=out_specs, out_shape=out_shape,
        compiler_params=_params("parallel", "parallel"),
        name="norm_proj",
    )(*args)


def _sb_weights(z, carry, tri, mask):
    drop = _softplus(z)
    if mask is not None:
        drop = jnp.where(mask, drop, 0.0)
    suffix = _dot(drop.astype(BF16), tri)
    a = jnp.exp(z - suffix - carry)
    if mask is not None:
        a = jnp.where(mask, a, 0.0)
    return a.astype(BF16), carry + jnp.sum(drop, axis=-1, keepdims=True)


def _softplus(z):
    return jnp.maximum(z, 0.0) + jnp.log(1.0 + jnp.exp(-jnp.abs(z)))


def _suffix_sum_matrix(n):
    r = lax.broadcasted_iota(jnp.int32, (n, n), 0)
    c = lax.broadcasted_iota(jnp.int32, (n, n), 1)
    return jnp.where(r >= c, 1.0, 0.0).astype(BF16)


def _prompt_item(first, last, q_ref, kt_ref, vt_ref, bias_ref, tri_ref, o_ref, acc_sc, carry_sc):
    tq = q_ref.shape[1]
    pairs = q_ref.shape[2] // LANES
    lane = lax.broadcasted_iota(jnp.int32, (tq, LANES), 1)
    tri = tri_ref[...]

    def run(mask, reset):
        for p in range(pairs):
            q2 = q_ref[0, :, p * LANES:(p + 1) * LANES]
            zero = jnp.zeros_like(q2)
            qbd = jnp.concatenate([jnp.where(lane < HEAD_DIM, q2, zero),
                                   jnp.where(lane >= HEAD_DIM, q2, zero)], axis=0)
            z = _dot(qbd, kt_ref[0, p, 0]) + bias_ref[p]
            carry = jnp.zeros((2 * tq, 1), F32) if reset else carry_sc[p]
            a, carry = _sb_weights(z, carry, tri, mask)
            o = _dot_nt(a, vt_ref[0, p, 0])
            acc_sc[p] = o if reset else acc_sc[p] + o
            carry_sc[p] = carry

    @pl.when(first)
    def _diagonal():
        row = lax.broadcasted_iota(jnp.int32, (2 * tq, tq), 0)
        col = lax.broadcasted_iota(jnp.int32, (2 * tq, tq), 1)
        run(col < jnp.where(row >= tq, row - tq, row), True)

    @pl.when(jnp.logical_not(first))
    def _earlier():
        run(None, False)

    @pl.when(last)
    def _store():
        for p in range(pairs):
            o_ref[0, :, p * LANES:(p + 1) * LANES] = jnp.where(
                lane < HEAD_DIM, acc_sc[p, :tq], acc_sc[p, tq:]).astype(o_ref.dtype)


SCHED_FIELDS = 8


def _prompt_schedule(nb, groups, nq, n_steps):
    items = [(b, hg, qi, kb) for b in range(nb) for hg in range(groups)
             for qi in range(nq) for kb in range(qi, -1, -1)]
    assert 0 < len(items) <= n_steps, "needs at least as many sample steps as prompt items"
    sched = np.zeros((n_steps, SCHED_FIELDS), np.int32)
    for i, (b, hg, qi, kb) in enumerate(items):
        sched[(i * n_steps) // len(items)] = (1, kb == qi, kb == 0, b, hg, qi, kb, 0)
    for s in range(1, n_steps):
        if not sched[s, 0]:
            sched[s, 3:] = sched[s - 1, 3:]
    return jnp.asarray(sched.reshape(-1))


PAGE_RING = 3


def _sb_kernel(pt_ref, sched_ref, q_ref, kn_ref, vn_ref, bias_ref, tri_ref, ck_hbm, cv_hbm,
               pq_ref, pkt_ref, pvt_ref, pbias_ref, ptri_ref, o_ref, po_ref,
               kbuf, vbuf, sems, qbd_sc, carry_sc, acc_new_sc, acc_t_sc, pacc_sc, pcarry_sc,
               *, pages_per_step, n_pages):
    n = pages_per_step
    steps = n_pages // n
    tq, width = q_ref.shape[1], q_ref.shape[2]
    heads = width // HEAD_DIM
    m = heads * tq
    bi, step = pl.program_id(0), pl.program_id(1)
    g = bi * steps + step
    n_steps = pl.num_programs(0) * steps
    slot = lax.rem(g, PAGE_RING)
    bias = bias_ref[...]
    tri = tri_ref[...]

    def page_copies(gi, buf):
        b_idx = lax.div(gi, steps)
        first_page = 2 * b_idx * n_pages + n_pages - 1 - gi * n
        copies = []
        for j in range(n):
            page = pt_ref[first_page - j]
            copies.append(pltpu.make_async_copy(ck_hbm.at[page], kbuf.at[buf, j], sems.at[0, buf]))
            copies.append(pltpu.make_async_copy(cv_hbm.at[page], vbuf.at[buf, j], sems.at[1, buf]))
        return copies

    def start_all(copies):
        for i, c in enumerate(copies):
            c.start(priority=i % 2)

    @pl.when(g == 0)
    def _first_fetches():
        for ahead in range(PAGE_RING - 1):
            @pl.when(ahead < n_steps)
            def _():
                start_all(page_copies(g + ahead, ahead))

    ahead = g + PAGE_RING - 1

    @pl.when(ahead < n_steps)
    def _prefetch():
        start_all(page_copies(ahead, lax.rem(ahead, PAGE_RING)))

    item = g * SCHED_FIELDS

    @pl.when(sched_ref[item] == 1)
    def _prompt():
        _prompt_item(sched_ref[item + 1] == 1, sched_ref[item + 2] == 1,
                     pq_ref, pkt_ref, pvt_ref, pbias_ref, ptri_ref, po_ref, pacc_sc, pcarry_sc)

    @pl.when(step == 0)
    def _new_tokens():
        qt = jnp.concatenate([q_ref[0]] * heads, axis=0)
        row = lax.broadcasted_iota(jnp.int32, (m, width), 0)
        lane = lax.broadcasted_iota(jnp.int32, (m, width), 1)
        qbd = jnp.where(_div_pow2(row, tq) == _head_of(lane), qt, 0.0).astype(BF16)
        qbd_sc[...] = qbd
        pad = jnp.zeros((PAGE - tq, width), F32)
        kn = jnp.concatenate([kn_ref[0], pad], axis=0).astype(BF16)
        vn = jnp.concatenate([vn_ref[0], pad], axis=0).astype(BF16)
        r = lax.broadcasted_iota(jnp.int32, (m, PAGE), 0)
        c = lax.broadcasted_iota(jnp.int32, (m, PAGE), 1)
        earlier_token = c < r - tq * _div_pow2(r, tq)
        a, carry = _sb_weights(_dot_nt(qbd, kn) + bias, jnp.zeros((m, 1), F32), tri, earlier_token)
        acc_new_sc[...] = _dot(a, vn)
        acc_t_sc[...] = jnp.zeros_like(acc_t_sc)
        carry_sc[...] = carry

    for c in page_copies(g, slot):
        c.wait()

    kt = jnp.concatenate([kbuf[slot, j] for j in range(n)], axis=1).astype(BF16)
    z = _dot(qbd_sc[...], kt) + bias
    drop = _softplus(z)
    stacked = jnp.concatenate([drop[:, j * PAGE:(j + 1) * PAGE] for j in range(n)], axis=0)
    suffix = _dot(stacked.astype(BF16), tri)
    totals = jnp.sum(stacked, axis=-1, keepdims=True)
    carry, weights = carry_sc[...], []
    for j in range(n):
        rows = slice(j * m, (j + 1) * m)
        weights.append(jnp.exp(z[:, j * PAGE:(j + 1) * PAGE] - suffix[rows] - carry))
        carry = carry + totals[rows]
    carry_sc[...] = carry
    a = jnp.concatenate(weights, axis=1)
    a = jnp.concatenate([a, jnp.zeros((LANES - m, n * PAGE), F32)], axis=0)
    vt = jnp.concatenate([vbuf[slot, j] for j in range(n)], axis=1).astype(BF16)
    acc_t_sc[...] += _dot(vt, a.T.astype(BF16))

    @pl.when(step == steps - 1)
    def _finish():
        acc = acc_new_sc[...] + acc_t_sc[...].T[:m]
        lane = lax.broadcasted_iota(jnp.int32, (tq, LANES), 1)
        pieces = []
        for j in range(width // LANES):
            lo = acc[2 * j * tq:(2 * j + 1) * tq, j * LANES:(j + 1) * LANES]
            hi = acc[(2 * j + 1) * tq:(2 * j + 2) * tq, j * LANES:(j + 1) * LANES]
            pieces.append(jnp.where(lane < HEAD_DIM, lo, hi))
        o_ref[0] = jnp.concatenate(pieces, axis=1).astype(o_ref.dtype)


def _sb_attention(q, k_new, v_new, cache_kt, cache_vt, page_table, bias_rows, tri,
                  pq, pkt, pvt, pbias_rows, ptri, pages_per_step, pairs_per_item):
    b, tq, width = q.shape
    nb, t, _ = pq.shape
    n_pages = page_table.shape[1]
    m = (width // HEAD_DIM) * tq
    pp, blk = pairs_per_item, PROMPT_BLOCK
    assert n_pages % pages_per_step == 0 and m <= LANES and width % (pp * LANES) == 0
    steps = n_pages // pages_per_step
    sched = _prompt_schedule(nb, width // (pp * LANES), t // blk, b * steps)

    def field(k):
        return lambda bi, s, pt, sc: sc[(bi * steps + s) * SCHED_FIELDS + k]

    pb, hg, qi, kb = field(3), field(4), field(5), field(6)
    tok = pl.BlockSpec((1, tq, width), lambda bi, s, pt, sc: (bi, 0, 0))
    pkv = pl.BlockSpec((1, pp, 1, LANES, blk), lambda *a: (pb(*a), hg(*a), kb(*a), 0, 0))
    pqo = pl.BlockSpec((1, blk, pp * LANES), lambda *a: (pb(*a), qi(*a), hg(*a)))
    page_buffers = pltpu.VMEM((PAGE_RING, pages_per_step, width, PAGE), F32)
    grid_spec = pltpu.PrefetchScalarGridSpec(
        num_scalar_prefetch=2,
        grid=(b, steps),
        in_specs=[tok, tok, tok,
                  pl.BlockSpec((m, 1), lambda bi, s, pt, sc: (0, 0)),
                  pl.BlockSpec((PAGE, PAGE), lambda bi, s, pt, sc: (0, 0)),
                  pl.BlockSpec(memory_space=pl.ANY),
                  pl.BlockSpec(memory_space=pl.ANY),
                  pqo, pkv, pkv,
                  pl.BlockSpec((pp, 2 * blk, 1), lambda *a: (hg(*a), 0, 0)),
                  pl.BlockSpec((blk, blk), lambda bi, s, pt, sc: (0, 0))],
        out_specs=[tok, pqo],
        scratch_shapes=[page_buffers, page_buffers,
                        pltpu.SemaphoreType.DMA((2, PAGE_RING)),
                        pltpu.VMEM((m, width), BF16),
                        pltpu.VMEM((m, 1), F32),
                        pltpu.VMEM((m, width), F32),
                        pltpu.VMEM((width, LANES), F32),
                        pltpu.VMEM((pp, 2 * blk, LANES), F32),
                        pltpu.VMEM((pp, 2 * blk, 1), F32)],
    )
    return pl.pallas_call(
        functools.partial(_sb_kernel, pages_per_step=pages_per_step, n_pages=n_pages),
        grid_spec=grid_spec,
        out_shape=[jax.ShapeDtypeStruct((b, tq, width), BF16), jax.ShapeDtypeStruct((nb, t, width), BF16)],
        compiler_params=_params("arbitrary", "arbitrary"),
        name="sb_attention",
    )(page_table.reshape(-1), sched, q, k_new, v_new, bias_rows, tri, cache_kt, cache_vt,
      pq, pkt, pvt, pbias_rows, ptri)


def _mem_attn_kernel(q_ref, kt_ref, vt_ref, o_ref):
    nb, tm, width = q_ref.shape
    heads = width // HEAD_DIM
    lane = lax.broadcasted_iota(jnp.int32, (tm, width), 1)
    for i in range(nb):
        q = q_ref[i]
        qbd = jnp.concatenate([jnp.where(_head_of(lane) == h, q, 0.0) for h in range(heads)],
                              axis=0).astype(BF16)
        s = _dot(qbd, kt_ref[i].astype(BF16))
        p = jnp.exp(s - jnp.max(s, axis=-1, keepdims=True))
        denom = jnp.sum(p, axis=-1, keepdims=True)
        o = _dot_nt(p.astype(BF16), vt_ref[i].astype(BF16)) / denom
        out = o[:tm]
        for h in range(1, heads):
            out = jnp.where(_head_of(lane) == h, o[h * tm:(h + 1) * tm], out)
        o_ref[i] = out.astype(o_ref.dtype)


def _mem_attn(q, kt, vt, tm, nb):
    b, t, width = q.shape
    mem = kt.shape[2]
    return pl.pallas_call(
        _mem_attn_kernel,
        grid=(b // nb, t // tm),
        in_specs=[pl.BlockSpec((nb, tm, width), lambda bi, i: (bi, i, 0)),
                  pl.BlockSpec((nb, width, mem), lambda bi, i: (bi, 0, 0)),
                  pl.BlockSpec((nb, width, mem), lambda bi, i: (bi, 0, 0))],
        out_specs=pl.BlockSpec((nb, tm, width), lambda bi, i: (bi, i, 0)),
        out_shape=jax.ShapeDtypeStruct((b, t, width), BF16),
        compiler_params=_params("parallel", "parallel"),
        name="mem_attn",
    )(q, kt, vt)


def _window_sums(ext, t, axis):
    def cut(x, lo, n):
        return lax.slice_in_dim(x, lo, lo + n, axis=axis)

    length = ext.shape[axis]
    sums, cur, span = [], ext, 1
    for w in POOL_WINDOWS:
        while span < w:
            n = cur.shape[axis] - span
            cur = cut(cur, span, n) + cut(cur, 0, n)
            span *= 2
        sums.append(cut(cur, length - t - w + 1, t))
    return sums


def _pool_finish(sums, u, pos, wbd_ref, scale_ref, o_ref):
    lane = lax.broadcasted_iota(jnp.int32, u.shape, 1)
    total, window = sums[0], jnp.full(u.shape, POOL_WINDOWS[0], jnp.int32)
    for g in range(1, len(POOL_WINDOWS)):
        sel = _head_of(lane) == g
        total = jnp.where(sel, sums[g], total)
        window = jnp.where(sel, POOL_WINDOWS[g], window)
    count = jnp.minimum(window, pos + 1).astype(F32)
    pooled = total / count - u
    o_ref[...] = (_dot(pooled.astype(BF16), wbd_ref[...]) * scale_ref[...]).astype(o_ref.dtype).reshape(o_ref.shape)


def _pool_prompt_kernel(u_ref, halo_ref, wbd_ref, scale_ref, o_ref):
    tm = u_ref.shape[1]
    i = pl.program_id(1)
    u = u_ref[0]
    halo = jnp.where(i > 0, halo_ref[0], 0.0)
    ext = jnp.concatenate([halo, u], axis=0)
    pos = i * tm + lax.broadcasted_iota(jnp.int32, u.shape, 0)
    _pool_finish(_window_sums(ext, tm, 0), u, pos, wbd_ref, scale_ref, o_ref)


def _pool_prompt(u, wbd, scale, tm):
    b, t, width = u.shape
    halo = 16
    return pl.pallas_call(
        _pool_prompt_kernel,
        grid=(b, t // tm),
        in_specs=[pl.BlockSpec((1, tm, width), lambda bi, i: (bi, i, 0)),
                  pl.BlockSpec((1, halo, width), lambda bi, i: (bi, jnp.maximum(i * (tm // halo) - 1, 0), 0)),
                  pl.BlockSpec((width, width), lambda bi, i: (0, 0)),
                  pl.BlockSpec((1, width), lambda bi, i: (0, 0))],
        out_specs=pl.BlockSpec((1, tm, width), lambda bi, i: (bi, i, 0)),
        out_shape=jax.ShapeDtypeStruct((b, t, width), BF16),
        compiler_params=_params("parallel", "parallel"),
        name="pool_prompt",
    )(u, u, wbd, scale)


def _pool_sample_kernel(ext_ref, wbd_ref, scale_ref, o_ref, *, pos0):
    t, b, width = o_ref.shape
    ext = ext_ref[...]
    sums = [s.reshape(t * b, width) for s in _window_sums(ext, t, 0)]
    u = ext[POOL_BUF:].reshape(t * b, width)
    pos = pos0 + lax.broadcasted_iota(jnp.int32, (t, b, width), 0).reshape(t * b, width)
    _pool_finish(sums, u, pos, wbd_ref, scale_ref, o_ref)


def _pool_sample(ext, wbd, scale, pos0):
    rows, b, width = ext.shape
    t = rows - POOL_BUF
    return pl.pallas_call(
        functools.partial(_pool_sample_kernel, pos0=pos0),
        out_shape=jax.ShapeDtypeStruct((t, b, width), BF16),
        compiler_params=pltpu.CompilerParams(vmem_limit_bytes=VMEM_LIMIT),
        name="pool_sample",
    )(ext, wbd, scale)


def _merge_kernel(x_ref, sb_ref, pool_ref, mem_ref, g_ref, wg_ref, bg_ref,
                  wsb_ref, wpool_ref, wmem_ref, wout_ref, h_ref):
    d = x_ref.shape[1]
    x = x_ref[...]
    xn = _rmsnorm(x, g_ref[...]).astype(BF16)
    merged = None
    for j, (o_ref, w_ref) in enumerate(((sb_ref, wsb_ref), (pool_ref, wpool_ref), (mem_ref, wmem_ref))):
        gate = jax.nn.sigmoid(_dot(xn, wg_ref[:, j * d:(j + 1) * d]) + bg_ref[:, j * d:(j + 1) * d])
        term = gate * _dot(o_ref[...], w_ref[...])
        merged = term if merged is None else merged + term
    h_ref[...] = x + _dot(merged.astype(BF16), wout_ref[...])


def _merge(x, o_sb, o_pool, o_mem, g, w_gate, b_gate, w_sb, w_pool, w_mem, w_out, tm):
    rows, d = x.shape

    def rows_spec(a):
        return pl.BlockSpec((tm, a.shape[1]), lambda i: (i, 0))

    g2, bg2 = g.reshape(1, d), b_gate.reshape(1, -1)
    consts = (g2, w_gate, bg2, w_sb, w_pool, w_mem, w_out)
    return pl.pallas_call(
        _merge_kernel,
        grid=(rows // tm,),
        in_specs=[rows_spec(a) for a in (x, o_sb, o_pool, o_mem)] + [_const_spec(a, 1) for a in consts],
        out_specs=pl.BlockSpec((tm, d), lambda i: (i, 0)),
        out_shape=jax.ShapeDtypeStruct((rows, d), F32),
        compiler_params=_params("parallel"),
        name="merge",
    )(x, o_sb, o_pool, o_mem, *consts)


def _conv_ffn_kernel(h_ref, hist0_ref, g_ref, wup_ref, cw_ref, cb_ref, wdown_ref, gfin_ref,
                     y_ref, hist_ref, hist_sc, act_sc, *, shift, final_norm):
    tm = h_ref.shape[1]
    hr = hist_sc.shape[0]
    d_ff = wdown_ref.shape[0]
    step = pl.program_id(1)

    @pl.when(step == 0)
    def _load_history():
        hist_sc[...] = hist0_ref[0]

    h = h_ref[0]
    hn = _rmsnorm(h, g_ref[...]).astype(BF16)
    for c in range(d_ff // MXU_COLS):
        halves = []
        for base in (c * MXU_COLS, d_ff + c * MXU_COLS):
            cols = slice(base, base + MXU_COLS)
            up = _dot(hn, wup_ref[:, cols])
            ext = jnp.concatenate([hist_sc[:, cols], up], axis=0)
            conv = cb_ref[:, cols]
            for j in range(CONV_WIDTH):
                lo = hr - (CONV_WIDTH - 1 - j) * shift
                conv = conv + ext[lo:lo + tm] * cw_ref[j:j + 1, cols]
            hist_sc[:, cols] = ext[tm:tm + hr]
            halves.append(conv)
        gate = halves[0]
        act = 0.5 * gate * (1.0 + lax.erf(gate * (2.0 ** -0.5))) * halves[1]
        act_sc[:, c * MXU_COLS:(c + 1) * MXU_COLS] = act.astype(BF16)
    y = h + _dot(act_sc[...], wdown_ref[...])
    if final_norm:
        y = _rmsnorm(y, gfin_ref[...])
    y_ref[0] = y

    @pl.when(step == pl.num_programs(1) - 1)
    def _store_history():
        hist_ref[0] = hist_sc[...]


def _conv_ffn(h, hist0, g, w_up, conv_w, conv_b, w_down, g_final, tm, shift, final_norm):
    b, t, d = h.shape
    hr, f = hist0.shape[1], hist0.shape[2]
    d_ff = w_down.shape[0]
    consts = (g.reshape(1, d), w_up, conv_w, conv_b.reshape(1, f), w_down, g_final.reshape(1, d))
    return pl.pallas_call(
        functools.partial(_conv_ffn_kernel, shift=shift, final_norm=final_norm),
        grid=(b, t // tm),
        in_specs=[pl.BlockSpec((1, tm, d), lambda bi, i: (bi, i, 0)),
                  pl.BlockSpec((1, hr, f), lambda bi, i: (bi, 0, 0))] + [_const_spec(a, 2) for a in consts],
        out_specs=[pl.BlockSpec((1, tm, d), lambda bi, i: (bi, i, 0)),
                   pl.BlockSpec((1, hr, f), lambda bi, i: (bi, 0, 0))],
        out_shape=[jax.ShapeDtypeStruct((b, t, d), F32), jax.ShapeDtypeStruct((b, hr, f), F32)],
        scratch_shapes=[pltpu.VMEM((hr, f), F32), pltpu.VMEM((tm, d_ff), BF16)],
        compiler_params=_params("parallel", "arbitrary"),
        name="conv_ffn",
    )(h, hist0, *consts)


def _bias_rows(bias, rows_per_head, heads_per_stack):
    return jnp.repeat(bias.astype(F32), rows_per_head).reshape(-1, heads_per_stack * rows_per_head, 1)


def _block_diag(w):
    groups, c, _ = w.shape
    eye = jnp.eye(groups, dtype=w.dtype)
    return (eye[:, None, :, None] * w[:, :, None, :]).reshape(groups * c, groups * c)


def _heads_last(xt, heads):
    lead, n = xt.shape[:-2], xt.shape[-1]
    xt = xt.reshape(*lead, heads, HEAD_DIM, n)
    return jnp.moveaxis(xt, -1, -3)


def _heads_first_t(x):
    b, n, heads, hd = x.shape
    return x.transpose(0, 2, 3, 1).reshape(b, heads * hd, n)


def kernel(x_prompt, x_sample, cache_k, cache_v, cache_mem_k, cache_mem_v, state_pool, state_ffn, page_table, mem_prompt, g_mix, w_in, sb_bias, w_gate, b_gate, w_pool, pool_scale, w_br_sb, w_br_pool, w_br_mem, w_out, g_mem, w_mem_kv, g_ffn, w_up, conv_w, conv_b, w_down, g_final):
    nb, seq, d = x_prompt.shape
    db, dseq, _ = x_sample.shape
    depth = w_in.shape[0]
    sb_w = w_br_sb.shape[1]
    pool_w = w_br_pool.shape[1]
    mem_w = w_br_mem.shape[1]
    f = w_up.shape[2]
    sb_heads = sb_w // HEAD_DIM
    mem_heads = mem_w // HEAD_DIM
    past_len = page_table.shape[1] * cache_k.shape[2]
    sb_scale = HEAD_DIM ** -0.5
    mem_scale = HEAD_DIM ** -0.5
    assert cache_k.shape[2] == PAGE and seq % (2 * PROMPT_BLOCK) == 0
    assert seq >= POOL_BUF and dseq >= CONV_WIDTH - 1

    hp, hs = x_prompt, x_sample
    outs = [[] for _ in range(10)]
    for l in range(depth):
        last = l == depth - 1
        w_in_l = w_in[l].astype(BF16)
        o_q, o_k, o_v, o_u, o_m = 0, sb_w, 2 * sb_w, 3 * sb_w, 3 * sb_w + pool_w
        w_kv_t = w_in_l[:, o_k:o_u].T
        wbd = _block_diag(w_pool[l]).astype(BF16)
        scale = pool_scale[l].reshape(1, pool_w)
        merge_w = (g_mix[l], w_gate[l].astype(BF16), b_gate[l], w_br_sb[l].astype(BF16),
                   w_br_pool[l].astype(BF16), w_br_mem[l].astype(BF16), w_out[l].astype(BF16))
        ffn_w = (g_ffn[l], w_up[l].astype(BF16), conv_w[l], conv_b[l], w_down[l].astype(BF16), g_final)

        mkt, mvt = _norm_proj(mem_prompt, g_mem[l], None, (), w_mem_kv[l].astype(BF16).T,
                              (mem_w, mem_w), False, tm=mem_prompt.shape[1])
        q, u, qm, kt, ktb, vt, vtb = _norm_proj(
            hp, g_mix[l], w_in_l,
            ((o_q, sb_w, BF16, sb_scale), (o_u, pool_w, F32, 1.0), (o_m, mem_w, F32, mem_scale)),
            w_kv_t, (sb_w, sb_w), True, tm=512)
        q_s, k_s, v_s, u_s, qm_s = (a.reshape(db, dseq, -1) for a in _norm_proj(
            hs.reshape(1, db * dseq, d), g_mix[l], w_in_l,
            ((o_q, sb_w, F32, sb_scale), (o_k, sb_w, F32, 1.0), (o_v, sb_w, F32, 1.0),
             (o_u, pool_w, F32, 1.0), (o_m, mem_w, F32, mem_scale)), None, (), False, tm=512))
        o_sb_s, o_sb = _sb_attention(
            q_s, k_s, v_s, _heads_first_t(cache_k[l]), _heads_first_t(cache_v[l]), page_table,
            _bias_rows(sb_bias[l], dseq, sb_heads)[0], _suffix_sum_matrix(PAGE),
            q, ktb, vtb, _bias_rows(sb_bias[l], PROMPT_BLOCK, 2), _suffix_sum_matrix(PROMPT_BLOCK),
            pages_per_step=16, pairs_per_item=sb_w // LANES)
        o_pool = _pool_prompt(u, wbd, scale, tm=512)
        o_mem = _mem_attn(qm, mkt, mvt, tm=256, nb=1)
        h = _merge(hp.reshape(nb * seq, d), o_sb.reshape(-1, sb_w), o_pool.reshape(-1, pool_w),
                   o_mem.reshape(-1, mem_w), *merge_w, tm=512)
        hist_rows = 8
        hp, hist = _conv_ffn(h.reshape(nb, seq, d), jnp.zeros((nb, hist_rows, f), F32), *ffn_w,
                             tm=512, shift=1, final_norm=last)
        for i, val in zip((0, 1, 4, 5, 6, 8),
                          (_heads_last(kt, sb_heads), _heads_last(vt, sb_heads),
                           _heads_last(mkt, mem_heads), _heads_last(mvt, mem_heads),
                           u[:, seq - POOL_BUF:], hist[:, hist_rows - (CONV_WIDTH - 1):])):
            outs[i].append(val)

        o_sb, k, v, u, qm = o_sb_s, k_s, v_s, u_s, qm_s
        ext = jnp.concatenate([state_pool[l].transpose(1, 0, 2), u.transpose(1, 0, 2)], axis=0)
        o_pool = _pool_sample(ext, wbd, scale, past_len).transpose(1, 0, 2)
        o_mem = _mem_attn(qm, _heads_first_t(cache_mem_k[l]), _heads_first_t(cache_mem_v[l]), tm=dseq, nb=8)
        h = _merge(hs.reshape(db * dseq, d), o_sb.reshape(-1, sb_w), o_pool.reshape(-1, pool_w),
                   o_mem.reshape(-1, mem_w), *merge_w, tm=512)
        h_t = h.reshape(db, dseq, d).transpose(1, 0, 2).reshape(1, dseq * db, d)
        hist0 = state_ffn[l].transpose(1, 0, 2).reshape(1, (CONV_WIDTH - 1) * db, f)
        y, hist = _conv_ffn(h_t, hist0, *ffn_w, tm=dseq * db, shift=db, final_norm=last)
        hs = y.reshape(dseq, db, d).transpose(1, 0, 2)
        for i, val in zip((2, 3, 7, 9),
                          (k.reshape(db, dseq, sb_heads, HEAD_DIM), v.reshape(db, dseq, sb_heads, HEAD_DIM),
                           ext[ext.shape[0] - POOL_BUF:].transpose(1, 0, 2),
                           hist.reshape(CONV_WIDTH - 1, db, f).transpose(1, 0, 2))):
            outs[i].append(val)

    k_p, v_p, k_s, v_s, mk_p, mv_p, pool_p, pool_s, ffn_p, ffn_s = (jnp.stack(o) for o in outs)
    return (hp, hs, k_p, v_p, k_s, v_s, mk_p, mv_p, pool_p, pool_s, ffn_p, ffn_s)
```

```python
import functools

import jax
import jax.numpy as jnp
import numpy as np
from jax import lax
from jax.experimental import pallas as pl
from jax.experimental.pallas import tpu as pltpu

F32 = jnp.float32
BF16 = jnp.bfloat16

NORM_EPS = 1e-6
LANES = 128
MXU_COLS = 256
HEAD_DIM = 64
POOL_WINDOWS = (2, 4, 8, 16)
POOL_BUF = max(POOL_WINDOWS) - 1
CONV_WIDTH = 3
PAGE = 128
PROMPT_BLOCK = 256
VMEM_LIMIT = 56 * 1024 * 1024


def _params(*sem):
    return pltpu.CompilerParams(dimension_semantics=sem, vmem_limit_bytes=VMEM_LIMIT)


def _rmsnorm(x, g):
    ms = jnp.mean(x * x, axis=-1, keepdims=True)
    return x * lax.rsqrt(ms + NORM_EPS) * g


def _div_pow2(x, n):
    assert n & (n - 1) == 0
    return lax.shift_right_logical(x, n.bit_length() - 1)


def _head_of(lane):
    return _div_pow2(lane, HEAD_DIM)


def _const_spec(a, grid_rank):
    index = {1: lambda i: (0,) * a.ndim, 2: lambda i, j: (0,) * a.ndim}[grid_rank]
    return pl.BlockSpec(a.shape, index, pipeline_mode=pl.Buffered(1))


def _dot(a, b):
    return jnp.dot(a, b, preferred_element_type=F32)


def _dot_nt(a, b):
    return lax.dot_general(a, b, (((1,), (1,)), ((), ())), preferred_element_type=F32)


def _norm_proj_kernel(*refs, cols, t_rows, blocked):
    refs = list(refs)
    x_ref, g_ref = refs.pop(0), refs.pop(0)
    w_ref = refs.pop(0) if cols else None
    wt_ref = refs.pop(0) if t_rows else None
    xn = _rmsnorm(x_ref[0], g_ref[...]).astype(BF16)
    for off, width, scale in cols:
        o_ref = refs.pop(0)
        y = _dot(xn, w_ref[:, off:off + width])
        if scale != 1.0:
            y = y * scale
        o_ref[0] = y.astype(o_ref.dtype)
    if t_rows:
        yt = _dot_nt(wt_ref[...], xn)
        off = 0
        for rows in t_rows:
            refs.pop(0)[0] = yt[off:off + rows]
            if blocked:
                b_ref = refs.pop(0)
                for r in range(b_ref.shape[1]):
                    for c in range(b_ref.shape[2]):
                        b_ref[0, r, c] = yt[off + r * LANES:off + (r + 1) * LANES,
                                            c * PROMPT_BLOCK:(c + 1) * PROMPT_BLOCK].astype(b_ref.dtype)
            off += rows


def _norm_proj(x, g, w, cols, wt, t_rows, blocked, tm):
    b, t, d = x.shape
    kern_cols = tuple((off, width, scale) for off, width, _, scale in cols)
    args, in_specs = [x, g.reshape(1, d)], [pl.BlockSpec((1, tm, d), lambda bi, i: (bi, i, 0)),
                                           _const_spec(g.reshape(1, d), 2)]
    for a in ((w,) if cols else ()) + ((wt,) if t_rows else ()):
        args.append(a)
        in_specs.append(_const_spec(a, 2))
    out_specs = [pl.BlockSpec((1, tm, width), lambda bi, i: (bi, i, 0)) for _, width, _, _ in cols]
    out_shape = [jax.ShapeDtypeStruct((b, t, width), dt) for _, width, dt, _ in cols]
    for rows in t_rows:
        out_specs.append(pl.BlockSpec((1, rows, tm), lambda bi, i: (bi, 0, i)))
        out_shape.append(jax.ShapeDtypeStruct((b, rows, t), F32))
        if blocked:
            nr, nc = rows // LANES, tm // PROMPT_BLOCK
            out_specs.append(pl.BlockSpec((1, nr, nc, LANES, PROMPT_BLOCK), lambda bi, i: (bi, 0, i, 0, 0)))
            out_shape.append(jax.ShapeDtypeStruct((b, nr, t // PROMPT_BLOCK, LANES, PROMPT_BLOCK), BF16))
    return pl.pallas_call(
        functools.partial(_norm_proj_kernel, cols=kern_cols, t_rows=tuple(t_rows), blocked=blocked),
        grid=(b, t // tm),
        in_specs=in_specs, out_specs=out_specs, out_shape=out_shape,
        compiler_params=_params("parallel", "parallel"),
        name="norm_proj",
    )(*args)


def _sb_weights(z, carry, sums, mask):
    n = z.shape[1]
    drop = _softplus(z)
    if mask is not None:
        drop = jnp.where(mask, drop, 0.0)
    summed = _dot(drop.astype(BF16), sums)
    a = jnp.exp(z - summed[:, :n] - _lanes(carry, n))
    if mask is not None:
        a = jnp.where(mask, a, 0.0)
    return a.astype(BF16), carry + summed[:, n:]


def _lanes(x, n):
    return x if n == LANES else jnp.concatenate([x] * (n // LANES), axis=1)


def _softplus(z):
    return jnp.maximum(z, 0.0) + jnp.log(1.0 + jnp.exp(-jnp.abs(z)))


def _sums_matrix(n):
    r = lax.broadcasted_iota(jnp.int32, (n, n + LANES), 0)
    c = lax.broadcasted_iota(jnp.int32, (n, n + LANES), 1)
    return jnp.where((c >= n) | (r >= c), 1.0, 0.0).astype(BF16)


def _prompt_item(first, last, group, q_ref, kt_ref, vt_ref, bias_ref, tri_ref, o_ref, acc_sc, carry_sc):
    tq = q_ref.shape[1]
    pairs = q_ref.shape[2] // LANES
    lane = lax.broadcasted_iota(jnp.int32, (tq, LANES), 1)
    tri = tri_ref[...]

    def run(mask, reset):
        for p in range(pairs):
            q2 = q_ref[0, :, p * LANES:(p + 1) * LANES]
            zero = jnp.zeros_like(q2)
            qbd = jnp.concatenate([jnp.where(lane < HEAD_DIM, q2, zero),
                                   jnp.where(lane >= HEAD_DIM, q2, zero)], axis=0)
            z = _dot(qbd, kt_ref[0, p, 0]) + _lanes(bias_ref[group * pairs + p], tq)
            carry = jnp.zeros((2 * tq, LANES), F32) if reset else carry_sc[p]
            a, carry = _sb_weights(z, carry, tri, mask)
            o = _dot_nt(a, vt_ref[0, p, 0])
            acc_sc[p] = o if reset else acc_sc[p] + o
            carry_sc[p] = carry

    @pl.when(first)
    def _diagonal():
        row = lax.broadcasted_iota(jnp.int32, (2 * tq, tq), 0)
        col = lax.broadcasted_iota(jnp.int32, (2 * tq, tq), 1)
        run(col < jnp.where(row >= tq, row - tq, row), True)

    @pl.when(jnp.logical_not(first))
    def _earlier():
        run(None, False)

    @pl.when(last)
    def _store():
        for p in range(pairs):
            o_ref[0, :, p * LANES:(p + 1) * LANES] = jnp.where(
                lane < HEAD_DIM, acc_sc[p, :tq], acc_sc[p, tq:]).astype(o_ref.dtype)


SCHED_FIELDS = 8


def _prompt_schedule(nb, groups, nq, n_steps):
    items = [(b, hg, qi, kb) for b in range(nb) for hg in range(groups)
             for qi in range(nq) for kb in range(qi, -1, -1)]
    assert 0 < len(items) <= n_steps, "needs at least as many sample steps as prompt items"
    sched = np.zeros((n_steps, SCHED_FIELDS), np.int32)
    for i, (b, hg, qi, kb) in enumerate(items):
        sched[(i * n_steps) // len(items)] = (1, kb == qi, kb == 0, b, hg, qi, kb, 0)
    for s in range(1, n_steps):
        if not sched[s, 0]:
            sched[s, 3:] = sched[s - 1, 3:]
    return jnp.asarray(sched.reshape(-1))


PAGE_RING = 3


def _sb_kernel(pt_ref, sched_ref, tok_ref, bias_ref, tri_ref, ck_hbm, cv_hbm,
               pq_ref, pkt_ref, pvt_ref, pbias_ref, ptri_ref, o_ref, po_ref,
               kbuf, vbuf, sems, qbd_sc, carry_sc, acc_new_sc, acc_t_sc, pacc_sc, pcarry_sc,
               *, pages_per_step, n_pages):
    n = pages_per_step
    steps = n_pages // n
    tq, width = tok_ref.shape[2], tok_ref.shape[3]
    heads = width // HEAD_DIM
    m = heads * tq
    bi, step = pl.program_id(0), pl.program_id(1)
    g = bi * steps + step
    n_steps = pl.num_programs(0) * steps
    slot = lax.rem(g, PAGE_RING)
    bias = bias_ref[...]
    tri = tri_ref[...]

    def page_copies(gi, buf):
        b_idx = lax.div(gi, steps)
        first_page = 2 * b_idx * n_pages + n_pages - 1 - gi * n
        copies = []
        for j in range(n):
            page = pt_ref[first_page - j]
            copies.append(pltpu.make_async_copy(ck_hbm.at[page], kbuf.at[buf, j], sems.at[0, buf]))
            copies.append(pltpu.make_async_copy(cv_hbm.at[page], vbuf.at[buf, j], sems.at[1, buf]))
        return copies

    def start_all(copies):
        for i, c in enumerate(copies):
            c.start(priority=i % 2)

    @pl.when(g == 0)
    def _first_fetches():
        for ahead in range(PAGE_RING - 1):
            @pl.when(ahead < n_steps)
            def _():
                start_all(page_copies(g + ahead, ahead))

    ahead = g + PAGE_RING - 1

    @pl.when(ahead < n_steps)
    def _prefetch():
        start_all(page_copies(ahead, lax.rem(ahead, PAGE_RING)))

    item = g * SCHED_FIELDS

    @pl.when(sched_ref[item] == 1)
    def _prompt():
        _prompt_item(sched_ref[item + 1] == 1, sched_ref[item + 2] == 1, sched_ref[item + 4],
                     pq_ref, pkt_ref, pvt_ref, pbias_ref, ptri_ref, po_ref, pacc_sc, pcarry_sc)

    @pl.when(step == 0)
    def _new_tokens():
        qt = jnp.concatenate([tok_ref[0, 0]] * heads, axis=0)
        row = lax.broadcasted_iota(jnp.int32, (m, width), 0)
        lane = lax.broadcasted_iota(jnp.int32, (m, width), 1)
        qbd = jnp.where(_div_pow2(row, tq) == _head_of(lane), qt, 0.0).astype(BF16)
        qbd_sc[...] = qbd
        pad = jnp.zeros((PAGE - tq, width), F32)
        kn = jnp.concatenate([tok_ref[1, 0], pad], axis=0).astype(BF16)
        vn = jnp.concatenate([tok_ref[2, 0], pad], axis=0).astype(BF16)
        r = lax.broadcasted_iota(jnp.int32, (m, PAGE), 0)
        c = lax.broadcasted_iota(jnp.int32, (m, PAGE), 1)
        earlier_token = c < r - tq * _div_pow2(r, tq)
        a, carry = _sb_weights(_dot_nt(qbd, kn) + bias, jnp.zeros((m, LANES), F32), tri, earlier_token)
        acc_new_sc[...] = _dot(a, vn)
        acc_t_sc[...] = jnp.zeros_like(acc_t_sc)
        carry_sc[...] = carry

    for c in page_copies(g, slot):
        c.wait()

    kt = jnp.concatenate([kbuf[slot, j] for j in range(n)], axis=1).astype(BF16)
    z = _dot(qbd_sc[...], kt) + _lanes(bias, n * PAGE)
    drop = _softplus(z)
    stacked = jnp.concatenate([drop[:, j * PAGE:(j + 1) * PAGE] for j in range(n)], axis=0)
    summed = _dot(stacked.astype(BF16), tri)
    carry, weights = carry_sc[...], []
    for j in range(n):
        rows = slice(j * m, (j + 1) * m)
        weights.append(jnp.exp(z[:, j * PAGE:(j + 1) * PAGE] - summed[rows, :PAGE] - carry))
        carry = carry + summed[rows, PAGE:]
    carry_sc[...] = carry
    a = jnp.concatenate(weights, axis=1)
    a = jnp.concatenate([a, jnp.zeros((LANES - m, n * PAGE), F32)], axis=0)
    vt = jnp.concatenate([vbuf[slot, j] for j in range(n)], axis=1).astype(BF16)
    acc_t_sc[...] += _dot(vt, a.T.astype(BF16))

    @pl.when(step == steps - 1)
    def _finish():
        acc = acc_new_sc[...] + acc_t_sc[...].T[:m]
        lane = lax.broadcasted_iota(jnp.int32, (tq, LANES), 1)
        pieces = []
        for j in range(width // LANES):
            lo = acc[2 * j * tq:(2 * j + 1) * tq, j * LANES:(j + 1) * LANES]
            hi = acc[(2 * j + 1) * tq:(2 * j + 2) * tq, j * LANES:(j + 1) * LANES]
            pieces.append(jnp.where(lane < HEAD_DIM, lo, hi))
        o_ref[0] = jnp.concatenate(pieces, axis=1).astype(o_ref.dtype)


def _sb_attention(q, k_new, v_new, cache_kt, cache_vt, page_table, bias_rows, tri,
                  pq, pkt, pvt, pbias_rows, ptri, pages_per_step, pairs_per_item):
    b, tq, width = q.shape
    nb, t, _ = pq.shape
    n_pages = page_table.shape[1]
    m = (width // HEAD_DIM) * tq
    pp, blk = pairs_per_item, PROMPT_BLOCK
    assert n_pages % pages_per_step == 0 and m <= LANES and width % (pp * LANES) == 0
    steps = n_pages // pages_per_step
    sched = _prompt_schedule(nb, width // (pp * LANES), t // blk, b * steps)

    def field(k):
        return lambda bi, s, pt, sc: sc[(bi * steps + s) * SCHED_FIELDS + k]

    pb, hg, qi, kb = field(3), field(4), field(5), field(6)
    tok = pl.BlockSpec((1, tq, width), lambda bi, s, pt, sc: (bi, 0, 0))
    pkv = pl.BlockSpec((1, pp, 1, LANES, blk), lambda *a: (pb(*a), hg(*a), kb(*a), 0, 0))
    pqo = pl.BlockSpec((1, blk, pp * LANES), lambda *a: (pb(*a), qi(*a), hg(*a)))
    page_buffers = pltpu.VMEM((PAGE_RING, pages_per_step, width, PAGE), F32)
    whole = pl.BlockSpec(memory_space=pltpu.VMEM)
    grid_spec = pltpu.PrefetchScalarGridSpec(
        num_scalar_prefetch=2,
        grid=(b, steps),
        in_specs=[pl.BlockSpec((3, 1, tq, width), lambda bi, s, pt, sc: (0, bi, 0, 0)), whole, whole,
                  pl.BlockSpec(memory_space=pl.ANY),
                  pl.BlockSpec(memory_space=pl.ANY),
                  pqo, pkv, pkv, whole, whole],
        out_specs=[tok, pqo],
        scratch_shapes=[page_buffers, page_buffers,
                        pltpu.SemaphoreType.DMA((2, PAGE_RING)),
                        pltpu.VMEM((m, width), BF16),
                        pltpu.VMEM((m, LANES), F32),
                        pltpu.VMEM((m, width), F32),
                        pltpu.VMEM((width, LANES), F32),
                        pltpu.VMEM((pp, 2 * blk, LANES), F32),
                        pltpu.VMEM((pp, 2 * blk, LANES), F32)],
    )
    return pl.pallas_call(
        functools.partial(_sb_kernel, pages_per_step=pages_per_step, n_pages=n_pages),
        grid_spec=grid_spec,
        out_shape=[jax.ShapeDtypeStruct((b, tq, width), BF16), jax.ShapeDtypeStruct((nb, t, width), BF16)],
        compiler_params=_params("arbitrary", "arbitrary"),
        name="sb_attention",
    )(page_table.reshape(-1), sched, jnp.stack([q, k_new, v_new]), bias_rows, tri, cache_kt, cache_vt,
      pq, pkt, pvt, pbias_rows, ptri)


def _mem_attn_kernel(q_ref, kt_ref, vt_ref, o_ref):
    nb, tm, width = q_ref.shape
    heads = width // HEAD_DIM
    lane = lax.broadcasted_iota(jnp.int32, (tm, width), 1)
    for i in range(nb):
        q = q_ref[i]
        qbd = jnp.concatenate([jnp.where(_head_of(lane) == h, q, 0.0) for h in range(heads)],
                              axis=0).astype(BF16)
        s = _dot(qbd, kt_ref[i].astype(BF16))
        p = jnp.exp(s - jnp.max(s, axis=-1, keepdims=True))
        denom = jnp.sum(p, axis=-1, keepdims=True)
        o = _dot_nt(p.astype(BF16), vt_ref[i].astype(BF16)) / denom
        out = o[:tm]
        for h in range(1, heads):
            out = jnp.where(_head_of(lane) == h, o[h * tm:(h + 1) * tm], out)
        o_ref[i] = out.astype(o_ref.dtype)


def _mem_attn(q, kt, vt, tm, nb):
    b, t, width = q.shape
    mem = kt.shape[2]
    return pl.pallas_call(
        _mem_attn_kernel,
        grid=(b // nb, t // tm),
        in_specs=[pl.BlockSpec((nb, tm, width), lambda bi, i: (bi, i, 0)),
                  pl.BlockSpec((nb, width, mem), lambda bi, i: (bi, 0, 0)),
                  pl.BlockSpec((nb, width, mem), lambda bi, i: (bi, 0, 0))],
        out_specs=pl.BlockSpec((nb, tm, width), lambda bi, i: (bi, i, 0)),
        out_shape=jax.ShapeDtypeStruct((b, t, width), BF16),
        compiler_params=_params("parallel", "parallel"),
        name="mem_attn",
    )(q, kt, vt)


def _window_sums(ext, t, axis):
    def cut(x, lo, n):
        return lax.slice_in_dim(x, lo, lo + n, axis=axis)

    length = ext.shape[axis]
    sums, cur, span = [], ext, 1
    for w in POOL_WINDOWS:
        while span < w:
            n = cur.shape[axis] - span
            cur = cut(cur, span, n) + cut(cur, 0, n)
            span *= 2
        sums.append(cut(cur, length - t - w + 1, t))
    return sums


def _pool_finish(sums, u, pos, wbd_ref, scale_ref, o_ref):
    lane = lax.broadcasted_iota(jnp.int32, u.shape, 1)
    total, window = sums[0], jnp.full(u.shape, POOL_WINDOWS[0], jnp.int32)
    for g in range(1, len(POOL_WINDOWS)):
        sel = _head_of(lane) == g
        total = jnp.where(sel, sums[g], total)
        window = jnp.where(sel, POOL_WINDOWS[g], window)
    count = jnp.minimum(window, pos + 1).astype(F32)
    pooled = total / count - u
    o_ref[...] = (_dot(pooled.astype(BF16), wbd_ref[...]) * scale_ref[...]).astype(o_ref.dtype).reshape(o_ref.shape)


def _pool_prompt_kernel(u_ref, halo_ref, wbd_ref, scale_ref, o_ref):
    tm = u_ref.shape[1]
    i = pl.program_id(1)
    u = u_ref[0]
    halo = jnp.where(i > 0, halo_ref[0], 0.0)
    ext = jnp.concatenate([halo, u], axis=0)
    pos = i * tm + lax.broadcasted_iota(jnp.int32, u.shape, 0)
    _pool_finish(_window_sums(ext, tm, 0), u, pos, wbd_ref, scale_ref, o_ref)


def _pool_prompt(u, wbd, scale, tm):
    b, t, width = u.shape
    halo = 16
    return pl.pallas_call(
        _pool_prompt_kernel,
        grid=(b, t // tm),
        in_specs=[pl.BlockSpec((1, tm, width), lambda bi, i: (bi, i, 0)),
                  pl.BlockSpec((1, halo, width), lambda bi, i: (bi, jnp.maximum(i * (tm // halo) - 1, 0), 0)),
                  pl.BlockSpec((width, width), lambda bi, i: (0, 0)),
                  pl.BlockSpec((1, width), lambda bi, i: (0, 0))],
        out_specs=pl.BlockSpec((1, tm, width), lambda bi, i: (bi, i, 0)),
        out_shape=jax.ShapeDtypeStruct((b, t, width), BF16),
        compiler_params=_params("parallel", "parallel"),
        name="pool_prompt",
    )(u, u, wbd, scale)


def _pool_sample_kernel(ext_ref, wbd_ref, scale_ref, o_ref, *, pos0):
    t, b, width = o_ref.shape
    ext = ext_ref[...]
    sums = [s.reshape(t * b, width) for s in _window_sums(ext, t, 0)]
    u = ext[POOL_BUF:].reshape(t * b, width)
    pos = pos0 + lax.broadcasted_iota(jnp.int32, (t, b, width), 0).reshape(t * b, width)
    _pool_finish(sums, u, pos, wbd_ref, scale_ref, o_ref)


def _pool_sample(ext, wbd, scale, pos0):
    rows, b, width = ext.shape
    t = rows - POOL_BUF
    return pl.pallas_call(
        functools.partial(_pool_sample_kernel, pos0=pos0),
        out_shape=jax.ShapeDtypeStruct((t, b, width), BF16),
        compiler_params=pltpu.CompilerParams(vmem_limit_bytes=VMEM_LIMIT),
        name="pool_sample",
    )(ext, wbd, scale)


def _merge_kernel(x_ref, sb_ref, pool_ref, mem_ref, g_ref, wg_ref, bg_ref,
                  wsb_ref, wpool_ref, wmem_ref, wout_ref, h_ref):
    d = x_ref.shape[1]
    x = x_ref[...]
    xn = _rmsnorm(x, g_ref[...]).astype(BF16)
    merged = None
    for j, (o_ref, w_ref) in enumerate(((sb_ref, wsb_ref), (pool_ref, wpool_ref), (mem_ref, wmem_ref))):
        gate = jax.nn.sigmoid(_dot(xn, wg_ref[:, j * d:(j + 1) * d]) + bg_ref[:, j * d:(j + 1) * d])
        term = gate * _dot(o_ref[...], w_ref[...])
        merged = term if merged is None else merged + term
    h_ref[...] = x + _dot(merged.astype(BF16), wout_ref[...])


def _merge(x, o_sb, o_pool, o_mem, g, w_gate, b_gate, w_sb, w_pool, w_mem, w_out, tm):
    rows, d = x.shape

    def rows_spec(a):
        return pl.BlockSpec((tm, a.shape[1]), lambda i: (i, 0))

    g2, bg2 = g.reshape(1, d), b_gate.reshape(1, -1)
    consts = (g2, w_gate, bg2, w_sb, w_pool, w_mem, w_out)
    return pl.pallas_call(
        _merge_kernel,
        grid=(rows // tm,),
        in_specs=[rows_spec(a) for a in (x, o_sb, o_pool, o_mem)] + [_const_spec(a, 1) for a in consts],
        out_specs=pl.BlockSpec((tm, d), lambda i: (i, 0)),
        out_shape=jax.ShapeDtypeStruct((rows, d), F32),
        compiler_params=_params("parallel"),
        name="merge",
    )(x, o_sb, o_pool, o_mem, *consts)


def _conv_ffn_kernel(h_ref, hist0_ref, g_ref, wup_ref, cw_ref, cb_ref, wdown_ref, gfin_ref,
                     y_ref, hist_ref, hist_sc, act_sc, *, shift, final_norm):
    tm = h_ref.shape[1]
    hr = hist_sc.shape[0]
    d_ff = wdown_ref.shape[0]
    step = pl.program_id(1)

    @pl.when(step == 0)
    def _load_history():
        hist_sc[...] = hist0_ref[0]

    h = h_ref[0]
    hn = _rmsnorm(h, g_ref[...]).astype(BF16)
    for c in range(d_ff // MXU_COLS):
        halves = []
        for base in (c * MXU_COLS, d_ff + c * MXU_COLS):
            cols = slice(base, base + MXU_COLS)
            up = _dot(hn, wup_ref[:, cols])
            ext = jnp.concatenate([hist_sc[:, cols], up], axis=0)
            conv = cb_ref[:, cols]
            for j in range(CONV_WIDTH):
                lo = hr - (CONV_WIDTH - 1 - j) * shift
                conv = conv + ext[lo:lo + tm] * cw_ref[j:j + 1, cols]
            hist_sc[:, cols] = ext[tm:tm + hr]
            halves.append(conv)
        gate = halves[0]
        act = 0.5 * gate * (1.0 + lax.erf(gate * (2.0 ** -0.5))) * halves[1]
        act_sc[:, c * MXU_COLS:(c + 1) * MXU_COLS] = act.astype(BF16)
    y = h + _dot(act_sc[...], wdown_ref[...])
    if final_norm:
        y = _rmsnorm(y, gfin_ref[...])
    y_ref[0] = y

    @pl.when(step == pl.num_programs(1) - 1)
    def _store_history():
        hist_ref[0] = hist_sc[...]


def _conv_ffn(h, hist0, g, w_up, conv_w, conv_b, w_down, g_final, tm, shift, final_norm):
    b, t, d = h.shape
    hr, f = hist0.shape[1], hist0.shape[2]
    d_ff = w_down.shape[0]
    consts = (g.reshape(1, d), w_up, conv_w, conv_b.reshape(1, f), w_down, g_final.reshape(1, d))
    return pl.pallas_call(
        functools.partial(_conv_ffn_kernel, shift=shift, final_norm=final_norm),
        grid=(b, t // tm),
        in_specs=[pl.BlockSpec((1, tm, d), lambda bi, i: (bi, i, 0)),
                  pl.BlockSpec((1, hr, f), lambda bi, i: (bi, 0, 0))] + [_const_spec(a, 2) for a in consts],
        out_specs=[pl.BlockSpec((1, tm, d), lambda bi, i: (bi, i, 0)),
                   pl.BlockSpec((1, hr, f), lambda bi, i: (bi, 0, 0))],
        out_shape=[jax.ShapeDtypeStruct((b, t, d), F32), jax.ShapeDtypeStruct((b, hr, f), F32)],
        scratch_shapes=[pltpu.VMEM((hr, f), F32), pltpu.VMEM((tm, d_ff), BF16)],
        compiler_params=_params("parallel", "arbitrary"),
        name="conv_ffn",
    )(h, hist0, *consts)


def _bias_rows(bias, rows_per_head, heads_per_stack):
    rows = jnp.repeat(bias.astype(F32), rows_per_head).reshape(-1, heads_per_stack * rows_per_head, 1)
    return jnp.broadcast_to(rows, rows.shape[:2] + (LANES,))


def _block_diag(w):
    groups, c, _ = w.shape
    eye = jnp.eye(groups, dtype=w.dtype)
    return (eye[:, None, :, None] * w[:, :, None, :]).reshape(groups * c, groups * c)


def _heads_last(xt, heads):
    lead, n = xt.shape[:-2], xt.shape[-1]
    xt = xt.reshape(*lead, heads, HEAD_DIM, n)
    return jnp.moveaxis(xt, -1, -3)


def _heads_first_t(x):
    b, n, heads, hd = x.shape
    return x.transpose(0, 2, 3, 1).reshape(b, heads * hd, n)


def kernel(x_prompt, x_sample, cache_k, cache_v, cache_mem_k, cache_mem_v, state_pool, state_ffn, page_table, mem_prompt, g_mix, w_in, sb_bias, w_gate, b_gate, w_pool, pool_scale, w_br_sb, w_br_pool, w_br_mem, w_out, g_mem, w_mem_kv, g_ffn, w_up, conv_w, conv_b, w_down, g_final):
    nb, seq, d = x_prompt.shape
    db, dseq, _ = x_sample.shape
    depth = w_in.shape[0]
    sb_w = w_br_sb.shape[1]
    pool_w = w_br_pool.shape[1]
    mem_w = w_br_mem.shape[1]
    f = w_up.shape[2]
    sb_heads = sb_w // HEAD_DIM
    mem_heads = mem_w // HEAD_DIM
    past_len = page_table.shape[1] * cache_k.shape[2]
    sb_scale = HEAD_DIM ** -0.5
    mem_scale = HEAD_DIM ** -0.5
    assert cache_k.shape[2] == PAGE and seq % (2 * PROMPT_BLOCK) == 0
    assert seq >= POOL_BUF and dseq >= CONV_WIDTH - 1

    hp, hs = x_prompt, x_sample
    outs = [[] for _ in range(10)]
    for l in range(depth):
        last = l == depth - 1
        w_in_l = w_in[l].astype(BF16)
        o_q, o_k, o_v, o_u, o_m = 0, sb_w, 2 * sb_w, 3 * sb_w, 3 * sb_w + pool_w
        w_kv_t = w_in_l[:, o_k:o_u].T
        wbd = _block_diag(w_pool[l]).astype(BF16)
        scale = pool_scale[l].reshape(1, pool_w)
        merge_w = (g_mix[l], w_gate[l].astype(BF16), b_gate[l], w_br_sb[l].astype(BF16),
                   w_br_pool[l].astype(BF16), w_br_mem[l].astype(BF16), w_out[l].astype(BF16))
        ffn_w = (g_ffn[l], w_up[l].astype(BF16), conv_w[l], conv_b[l], w_down[l].astype(BF16), g_final)

        mkt, mvt = _norm_proj(mem_prompt, g_mem[l], None, (), w_mem_kv[l].astype(BF16).T,
                              (mem_w, mem_w), False, tm=mem_prompt.shape[1])
        q, u, qm, kt, ktb, vt, vtb = _norm_proj(
            hp, g_mix[l], w_in_l,
            ((o_q, sb_w, BF16, sb_scale), (o_u, pool_w, F32, 1.0), (o_m, mem_w, F32, mem_scale)),
            w_kv_t, (sb_w, sb_w), True, tm=512)
        q_s, k_s, v_s, u_s, qm_s = (a.reshape(db, dseq, -1) for a in _norm_proj(
            hs.reshape(1, db * dseq, d), g_mix[l], w_in_l,
            ((o_q, sb_w, F32, sb_scale), (o_k, sb_w, F32, 1.0), (o_v, sb_w, F32, 1.0),
             (o_u, pool_w, F32, 1.0), (o_m, mem_w, F32, mem_scale)), None, (), False, tm=512))
        o_sb_s, o_sb = _sb_attention(
            q_s, k_s, v_s, _heads_first_t(cache_k[l]), _heads_first_t(cache_v[l]), page_table,
            _bias_rows(sb_bias[l], dseq, sb_heads)[0], _sums_matrix(PAGE),
            q, ktb, vtb, _bias_rows(sb_bias[l], PROMPT_BLOCK, 2), _sums_matrix(PROMPT_BLOCK),
            pages_per_step=16, pairs_per_item=sb_w // LANES)
        o_pool = _pool_prompt(u, wbd, scale, tm=512)
        o_mem = _mem_attn(qm, mkt, mvt, tm=256, nb=1)
        h = _merge(hp.reshape(nb * seq, d), o_sb.reshape(-1, sb_w), o_pool.reshape(-1, pool_w),
                   o_mem.reshape(-1, mem_w), *merge_w, tm=512)
        hist_rows = 8
        hp, hist = _conv_ffn(h.reshape(nb, seq, d), jnp.zeros((nb, hist_rows, f), F32), *ffn_w,
                             tm=512, shift=1, final_norm=last)
        for i, val in zip((0, 1, 4, 5, 6, 8),
                          (_heads_last(kt, sb_heads), _heads_last(vt, sb_heads),
                           _heads_last(mkt, mem_heads), _heads_last(mvt, mem_heads),
                           u[:, seq - POOL_BUF:], hist[:, hist_rows - (CONV_WIDTH - 1):])):
            outs[i].append(val)

        o_sb, k, v, u, qm = o_sb_s, k_s, v_s, u_s, qm_s
        ext = jnp.concatenate([state_pool[l].transpose(1, 0, 2), u.transpose(1, 0, 2)], axis=0)
        o_pool = _pool_sample(ext, wbd, scale, past_len).transpose(1, 0, 2)
        o_mem = _mem_attn(qm, _heads_first_t(cache_mem_k[l]), _heads_first_t(cache_mem_v[l]), tm=dseq, nb=8)
        h = _merge(hs.reshape(db * dseq, d), o_sb.reshape(-1, sb_w), o_pool.reshape(-1, pool_w),
                   o_mem.reshape(-1, mem_w), *merge_w, tm=512)
        h_t = h.reshape(db, dseq, d).transpose(1, 0, 2).reshape(1, dseq * db, d)
        hist0 = state_ffn[l].transpose(1, 0, 2).reshape(1, (CONV_WIDTH - 1) * db, f)
        y, hist = _conv_ffn(h_t, hist0, *ffn_w, tm=dseq * db, shift=db, final_norm=last)
        hs = y.reshape(dseq, db, d).transpose(1, 0, 2)
        for i, val in zip((2, 3, 7, 9),
                          (k.reshape(db, dseq, sb_heads, HEAD_DIM), v.reshape(db, dseq, sb_heads, HEAD_DIM),
                           ext[ext.shape[0] - POOL_BUF:].transpose(1, 0, 2),
                           hist.reshape(CONV_WIDTH - 1, db, f).transpose(1, 0, 2))):
            outs[i].append(val)

    k_p, v_p, k_s, v_s, mk_p, mv_p, pool_p, pool_s, ffn_p, ffn_s = (jnp.stack(o) for o in outs)
    return (hp, hs, k_p, v_p, k_s, v_s, mk_p, mv_p, pool_p, pool_s, ffn_p, ffn_s)
```

```python
import functools

import jax
import jax.numpy as jnp
import numpy as np
from jax import lax
from jax.experimental import pallas as pl
from jax.experimental.pallas import tpu as pltpu

F32 = jnp.float32
BF16 = jnp.bfloat16

NORM_EPS = 1e-6
LANES = 128
MXU_COLS = 256
HEAD_DIM = 64
POOL_WINDOWS = (2, 4, 8, 16)
POOL_BUF = max(POOL_WINDOWS) - 1
CONV_WIDTH = 3
PAGE = 128
PROMPT_BLOCK = 256
VMEM_LIMIT = 56 * 1024 * 1024
ROW_TILE = 1024


def _row_tile(rows, tile=ROW_TILE):
    tile = min(tile, rows)
    assert rows % tile == 0
    return tile


def _params(*sem):
    return pltpu.CompilerParams(dimension_semantics=sem, vmem_limit_bytes=VMEM_LIMIT)


def _rmsnorm(x, g):
    ms = jnp.mean(x * x, axis=-1, keepdims=True)
    return x * lax.rsqrt(ms + NORM_EPS) * g


def _div_pow2(x, n):
    assert n & (n - 1) == 0
    return lax.shift_right_logical(x, n.bit_length() - 1)


def _head_of(lane):
    return _div_pow2(lane, HEAD_DIM)


def _const_spec(a, grid_rank):
    index = {1: lambda i: (0,) * a.ndim, 2: lambda i, j: (0,) * a.ndim}[grid_rank]
    return pl.BlockSpec(a.shape, index, pipeline_mode=pl.Buffered(1))


def _dot(a, b):
    return jnp.dot(a, b, preferred_element_type=F32)


def _dot_nt(a, b):
    return lax.dot_general(a, b, (((1,), (1,)), ((), ())), preferred_element_type=F32)


def _norm_proj_kernel(*refs, cols, t_rows, blocked):
    refs = list(refs)
    x_ref, g_ref = refs.pop(0), refs.pop(0)
    w_ref = refs.pop(0) if cols else None
    wt_ref = refs.pop(0) if t_rows else None
    xn = _rmsnorm(x_ref[0], g_ref[...]).astype(BF16)
    for off, width, scale in cols:
        o_ref = refs.pop(0)
        y = _dot(xn, w_ref[:, off:off + width])
        if scale != 1.0:
            y = y * scale
        o_ref[0] = y.astype(o_ref.dtype)
    if t_rows:
        yt = _dot_nt(wt_ref[...], xn)
        off = 0
        for rows in t_rows:
            refs.pop(0)[0] = yt[off:off + rows]
            if blocked:
                b_ref = refs.pop(0)
                for r in range(b_ref.shape[1]):
                    for c in range(b_ref.shape[2]):
                        b_ref[0, r, c] = yt[off + r * LANES:off + (r + 1) * LANES,
                                            c * PROMPT_BLOCK:(c + 1) * PROMPT_BLOCK].astype(b_ref.dtype)
            off += rows


def _norm_proj(x, g, w, cols, wt, t_rows, blocked, tm):
    b, t, d = x.shape
    kern_cols = tuple((off, width, scale) for off, width, _, scale in cols)
    args, in_specs = [x, g.reshape(1, d)], [pl.BlockSpec((1, tm, d), lambda bi, i: (bi, i, 0)),
                                           _const_spec(g.reshape(1, d), 2)]
    for a in ((w,) if cols else ()) + ((wt,) if t_rows else ()):
        args.append(a)
        in_specs.append(_const_spec(a, 2))
    out_specs = [pl.BlockSpec((1, tm, width), lambda bi, i: (bi, i, 0)) for _, width, _, _ in cols]
    out_shape = [jax.ShapeDtypeStruct((b, t, width), dt) for _, width, dt, _ in cols]
    for rows in t_rows:
        out_specs.append(pl.BlockSpec((1, rows, tm), lambda bi, i: (bi, 0, i)))
        out_shape.append(jax.ShapeDtypeStruct((b, rows, t), F32))
        if blocked:
            nr, nc = rows // LANES, tm // PROMPT_BLOCK
            out_specs.append(pl.BlockSpec((1, nr, nc, LANES, PROMPT_BLOCK), lambda bi, i: (bi, 0, i, 0, 0)))
            out_shape.append(jax.ShapeDtypeStruct((b, nr, t // PROMPT_BLOCK, LANES, PROMPT_BLOCK), BF16))
    return pl.pallas_call(
        functools.partial(_norm_proj_kernel, cols=kern_cols, t_rows=tuple(t_rows), blocked=blocked),
        grid=(b, t // tm),
        in_specs=in_specs, out_specs=out_specs, out_shape=out_shape,
        compiler_params=_params("parallel", "parallel"),
        name="norm_proj",
    )(*args)


def _sb_weights(z, carry, sums, mask):
    n = z.shape[1]
    drop = _softplus(z)
    if mask is not None:
        drop = jnp.where(mask, drop, 0.0)
    summed = _dot(drop.astype(BF16), sums)
    a = jnp.exp(z - summed[:, :n] - _lanes(carry, n))
    if mask is not None:
        a = jnp.where(mask, a, 0.0)
    return a.astype(BF16), carry + summed[:, n:]


def _lanes(x, n):
    return x if n == LANES else jnp.concatenate([x] * (n // LANES), axis=1)


def _softplus(z):
    return jnp.maximum(z, 0.0) + jnp.log(1.0 + jnp.exp(-jnp.abs(z)))


def _sums_matrix(n):
    r = lax.broadcasted_iota(jnp.int32, (n, n + LANES), 0)
    c = lax.broadcasted_iota(jnp.int32, (n, n + LANES), 1)
    return jnp.where((c >= n) | (r >= c), 1.0, 0.0).astype(BF16)


def _prompt_item(first, last, group, q_ref, kt_ref, vt_ref, bias_ref, tri_ref, o_ref, acc_sc, carry_sc):
    tq = q_ref.shape[1]
    pairs = q_ref.shape[2] // LANES
    lane = lax.broadcasted_iota(jnp.int32, (tq, LANES), 1)
    tri = tri_ref[...]

    def run(mask, reset):
        for p in range(pairs):
            q2 = q_ref[0, :, p * LANES:(p + 1) * LANES]
            zero = jnp.zeros_like(q2)
            qbd = jnp.concatenate([jnp.where(lane < HEAD_DIM, q2, zero),
                                   jnp.where(lane >= HEAD_DIM, q2, zero)], axis=0)
            z = _dot(qbd, kt_ref[0, p, 0]) + _lanes(bias_ref[group * pairs + p], tq)
            carry = jnp.zeros((2 * tq, LANES), F32) if reset else carry_sc[p]
            a, carry = _sb_weights(z, carry, tri, mask)
            o = _dot_nt(a, vt_ref[0, p, 0])
            acc_sc[p] = o if reset else acc_sc[p] + o
            carry_sc[p] = carry

    @pl.when(first)
    def _diagonal():
        row = lax.broadcasted_iota(jnp.int32, (2 * tq, tq), 0)
        col = lax.broadcasted_iota(jnp.int32, (2 * tq, tq), 1)
        run(col < jnp.where(row >= tq, row - tq, row), True)

    @pl.when(jnp.logical_not(first))
    def _earlier():
        run(None, False)

    @pl.when(last)
    def _store():
        for p in range(pairs):
            o_ref[0, :, p * LANES:(p + 1) * LANES] = jnp.where(
                lane < HEAD_DIM, acc_sc[p, :tq], acc_sc[p, tq:]).astype(o_ref.dtype)


SCHED_FIELDS = 8


def _prompt_schedule(nb, groups, nq, n_steps):
    items = [(b, hg, qi, kb) for b in range(nb) for hg in range(groups)
             for qi in range(nq) for kb in range(qi, -1, -1)]
    assert 0 < len(items) <= n_steps, "needs at least as many sample steps as prompt items"
    sched = np.zeros((n_steps, SCHED_FIELDS), np.int32)
    for i, (b, hg, qi, kb) in enumerate(items):
        sched[(i * n_steps) // len(items)] = (1, kb == qi, kb == 0, b, hg, qi, kb, 0)
    for s in range(1, n_steps):
        if not sched[s, 0]:
            sched[s, 3:] = sched[s - 1, 3:]
    return jnp.asarray(sched.reshape(-1))


PAGE_RING = 3


def _sb_kernel(pt_ref, sched_ref, tok_ref, bias_ref, tri_ref, ck_hbm, cv_hbm,
               pq_ref, pkt_ref, pvt_ref, pbias_ref, ptri_ref, o_ref, po_ref,
               kbuf, vbuf, sems, qbd_sc, carry_sc, acc_new_sc, acc_t_sc, pacc_sc, pcarry_sc,
               *, pages_per_step, n_pages):
    n = pages_per_step
    steps = n_pages // n
    tq, width = tok_ref.shape[2], tok_ref.shape[3]
    heads = width // HEAD_DIM
    m = heads * tq
    bi, step = pl.program_id(0), pl.program_id(1)
    g = bi * steps + step
    n_steps = pl.num_programs(0) * steps
    slot = lax.rem(g, PAGE_RING)
    bias = bias_ref[...]
    tri = tri_ref[...]

    def page_copies(gi, buf):
        b_idx = lax.div(gi, steps)
        first_page = 2 * b_idx * n_pages + n_pages - 1 - gi * n
        copies = []
        for j in range(n):
            page = pt_ref[first_page - j]
            copies.append(pltpu.make_async_copy(ck_hbm.at[page], kbuf.at[buf, j], sems.at[0, buf]))
            copies.append(pltpu.make_async_copy(cv_hbm.at[page], vbuf.at[buf, j], sems.at[1, buf]))
        return copies

    def start_all(copies):
        for i, c in enumerate(copies):
            c.start(priority=i % 2)

    @pl.when(g == 0)
    def _first_fetches():
        for ahead in range(PAGE_RING - 1):
            @pl.when(ahead < n_steps)
            def _():
                start_all(page_copies(g + ahead, ahead))

    ahead = g + PAGE_RING - 1

    @pl.when(ahead < n_steps)
    def _prefetch():
        start_all(page_copies(ahead, lax.rem(ahead, PAGE_RING)))

    item = g * SCHED_FIELDS

    @pl.when(sched_ref[item] == 1)
    def _prompt():
        _prompt_item(sched_ref[item + 1] == 1, sched_ref[item + 2] == 1, sched_ref[item + 4],
                     pq_ref, pkt_ref, pvt_ref, pbias_ref, ptri_ref, po_ref, pacc_sc, pcarry_sc)

    @pl.when(step == 0)
    def _new_tokens():
        qt = jnp.concatenate([tok_ref[0, 0]] * heads, axis=0)
        row = lax.broadcasted_iota(jnp.int32, (m, width), 0)
        lane = lax.broadcasted_iota(jnp.int32, (m, width), 1)
        qbd = jnp.where(_div_pow2(row, tq) == _head_of(lane), qt, 0.0).astype(BF16)
        qbd_sc[...] = qbd
        pad = jnp.zeros((PAGE - tq, width), F32)
        kn = jnp.concatenate([tok_ref[1, 0], pad], axis=0).astype(BF16)
        vn = jnp.concatenate([tok_ref[2, 0], pad], axis=0).astype(BF16)
        r = lax.broadcasted_iota(jnp.int32, (m, PAGE), 0)
        c = lax.broadcasted_iota(jnp.int32, (m, PAGE), 1)
        earlier_token = c < r - tq * _div_pow2(r, tq)
        a, carry = _sb_weights(_dot_nt(qbd, kn) + bias, jnp.zeros((m, LANES), F32), tri, earlier_token)
        acc_new_sc[...] = _dot(a, vn)
        acc_t_sc[...] = jnp.zeros_like(acc_t_sc)
        carry_sc[...] = carry

    for c in page_copies(g, slot):
        c.wait()

    kt = jnp.concatenate([kbuf[slot, j] for j in range(n)], axis=1).astype(BF16)
    z = _dot(qbd_sc[...], kt) + _lanes(bias, n * PAGE)
    drop = _softplus(z)
    stacked = jnp.concatenate([drop[:, j * PAGE:(j + 1) * PAGE] for j in range(n)], axis=0)
    summed = _dot(stacked.astype(BF16), tri)
    carry, weights = carry_sc[...], []
    for j in range(n):
        rows = slice(j * m, (j + 1) * m)
        weights.append(jnp.exp(z[:, j * PAGE:(j + 1) * PAGE] - summed[rows, :PAGE] - carry))
        carry = carry + summed[rows, PAGE:]
    carry_sc[...] = carry
    a = jnp.concatenate(weights, axis=1)
    a = jnp.concatenate([a, jnp.zeros((LANES - m, n * PAGE), F32)], axis=0)
    vt = jnp.concatenate([vbuf[slot, j] for j in range(n)], axis=1).astype(BF16)
    acc_t_sc[...] += _dot(vt, a.T.astype(BF16))

    @pl.when(step == steps - 1)
    def _finish():
        acc = acc_new_sc[...] + acc_t_sc[...].T[:m]
        lane = lax.broadcasted_iota(jnp.int32, (tq, LANES), 1)
        pieces = []
        for j in range(width // LANES):
            lo = acc[2 * j * tq:(2 * j + 1) * tq, j * LANES:(j + 1) * LANES]
            hi = acc[(2 * j + 1) * tq:(2 * j + 2) * tq, j * LANES:(j + 1) * LANES]
            pieces.append(jnp.where(lane < HEAD_DIM, lo, hi))
        o_ref[0] = jnp.concatenate(pieces, axis=1).astype(o_ref.dtype)


def _sb_attention(q, k_new, v_new, cache_kt, cache_vt, page_table, bias_rows, tri,
                  pq, pkt, pvt, pbias_rows, ptri, pages_per_step, pairs_per_item):
    b, tq, width = q.shape
    nb, t, _ = pq.shape
    n_pages = page_table.shape[1]
    m = (width // HEAD_DIM) * tq
    pp, blk = pairs_per_item, PROMPT_BLOCK
    assert n_pages % pages_per_step == 0 and m <= LANES and width % (pp * LANES) == 0
    steps = n_pages // pages_per_step
    sched = _prompt_schedule(nb, width // (pp * LANES), t // blk, b * steps)

    def field(k):
        return lambda bi, s, pt, sc: sc[(bi * steps + s) * SCHED_FIELDS + k]

    pb, hg, qi, kb = field(3), field(4), field(5), field(6)
    tok = pl.BlockSpec((1, tq, width), lambda bi, s, pt, sc: (bi, 0, 0))
    pkv = pl.BlockSpec((1, pp, 1, LANES, blk), lambda *a: (pb(*a), hg(*a), kb(*a), 0, 0))
    pqo = pl.BlockSpec((1, blk, pp * LANES), lambda *a: (pb(*a), qi(*a), hg(*a)))
    page_buffers = pltpu.VMEM((PAGE_RING, pages_per_step, width, PAGE), F32)
    whole = pl.BlockSpec(memory_space=pltpu.VMEM)
    grid_spec = pltpu.PrefetchScalarGridSpec(
        num_scalar_prefetch=2,
        grid=(b, steps),
        in_specs=[pl.BlockSpec((3, 1, tq, width), lambda bi, s, pt, sc: (0, bi, 0, 0)), whole, whole,
                  pl.BlockSpec(memory_space=pl.ANY),
                  pl.BlockSpec(memory_space=pl.ANY),
                  pqo, pkv, pkv, whole, whole],
        out_specs=[tok, pqo],
        scratch_shapes=[page_buffers, page_buffers,
                        pltpu.SemaphoreType.DMA((2, PAGE_RING)),
                        pltpu.VMEM((m, width), BF16),
                        pltpu.VMEM((m, LANES), F32),
                        pltpu.VMEM((m, width), F32),
                        pltpu.VMEM((width, LANES), F32),
                        pltpu.VMEM((pp, 2 * blk, LANES), F32),
                        pltpu.VMEM((pp, 2 * blk, LANES), F32)],
    )
    return pl.pallas_call(
        functools.partial(_sb_kernel, pages_per_step=pages_per_step, n_pages=n_pages),
        grid_spec=grid_spec,
        out_shape=[jax.ShapeDtypeStruct((b, tq, width), BF16), jax.ShapeDtypeStruct((nb, t, width), BF16)],
        compiler_params=_params("arbitrary", "arbitrary"),
        name="sb_attention",
    )(page_table.reshape(-1), sched, jnp.stack([q, k_new, v_new]), bias_rows, tri, cache_kt, cache_vt,
      pq, pkt, pvt, pbias_rows, ptri)


def _mem_attn_kernel(q_ref, kt_ref, vt_ref, o_ref):
    nb, tm, width = q_ref.shape
    heads = width // HEAD_DIM
    lane = lax.broadcasted_iota(jnp.int32, (tm, width), 1)
    for i in range(nb):
        q = q_ref[i]
        qbd = jnp.concatenate([jnp.where(_head_of(lane) == h, q, 0.0) for h in range(heads)],
                              axis=0).astype(BF16)
        s = _dot(qbd, kt_ref[i].astype(BF16))
        p = jnp.exp(s - jnp.max(s, axis=-1, keepdims=True))
        denom = jnp.sum(p, axis=-1, keepdims=True)
        o = _dot_nt(p.astype(BF16), vt_ref[i].astype(BF16)) / denom
        out = o[:tm]
        for h in range(1, heads):
            out = jnp.where(_head_of(lane) == h, o[h * tm:(h + 1) * tm], out)
        o_ref[i] = out.astype(o_ref.dtype)


def _mem_attn(q, kt, vt, tm, nb):
    b, t, width = q.shape
    mem = kt.shape[2]
    return pl.pallas_call(
        _mem_attn_kernel,
        grid=(b // nb, t // tm),
        in_specs=[pl.BlockSpec((nb, tm, width), lambda bi, i: (bi, i, 0)),
                  pl.BlockSpec((nb, width, mem), lambda bi, i: (bi, 0, 0)),
                  pl.BlockSpec((nb, width, mem), lambda bi, i: (bi, 0, 0))],
        out_specs=pl.BlockSpec((nb, tm, width), lambda bi, i: (bi, i, 0)),
        out_shape=jax.ShapeDtypeStruct((b, t, width), BF16),
        compiler_params=_params("parallel", "parallel"),
        name="mem_attn",
    )(q, kt, vt)


def _window_sums(ext, t, axis):
    def cut(x, lo, n):
        return lax.slice_in_dim(x, lo, lo + n, axis=axis)

    length = ext.shape[axis]
    sums, cur, span = [], ext, 1
    for w in POOL_WINDOWS:
        while span < w:
            n = cur.shape[axis] - span
            cur = cut(cur, span, n) + cut(cur, 0, n)
            span *= 2
        sums.append(cut(cur, length - t - w + 1, t))
    return sums


def _pool_finish(sums, u, pos, wbd_ref, scale_ref, o_ref):
    lane = lax.broadcasted_iota(jnp.int32, u.shape, 1)
    total, window = sums[0], jnp.full(u.shape, POOL_WINDOWS[0], jnp.int32)
    for g in range(1, len(POOL_WINDOWS)):
        sel = _head_of(lane) == g
        total = jnp.where(sel, sums[g], total)
        window = jnp.where(sel, POOL_WINDOWS[g], window)
    count = jnp.minimum(window, pos + 1).astype(F32)
    pooled = total / count - u
    o_ref[...] = (_dot(pooled.astype(BF16), wbd_ref[...]) * scale_ref[...]).astype(o_ref.dtype).reshape(o_ref.shape)


def _pool_prompt_kernel(u_ref, halo_ref, wbd_ref, scale_ref, o_ref):
    tm = u_ref.shape[1]
    i = pl.program_id(1)
    u = u_ref[0]
    halo = jnp.where(i > 0, halo_ref[0], 0.0)
    ext = jnp.concatenate([halo, u], axis=0)
    pos = i * tm + lax.broadcasted_iota(jnp.int32, u.shape, 0)
    _pool_finish(_window_sums(ext, tm, 0), u, pos, wbd_ref, scale_ref, o_ref)


def _pool_prompt(u, wbd, scale, tm):
    b, t, width = u.shape
    halo = 16
    return pl.pallas_call(
        _pool_prompt_kernel,
        grid=(b, t // tm),
        in_specs=[pl.BlockSpec((1, tm, width), lambda bi, i: (bi, i, 0)),
                  pl.BlockSpec((1, halo, width), lambda bi, i: (bi, jnp.maximum(i * (tm // halo) - 1, 0), 0)),
                  pl.BlockSpec((width, width), lambda bi, i: (0, 0)),
                  pl.BlockSpec((1, width), lambda bi, i: (0, 0))],
        out_specs=pl.BlockSpec((1, tm, width), lambda bi, i: (bi, i, 0)),
        out_shape=jax.ShapeDtypeStruct((b, t, width), BF16),
        compiler_params=_params("parallel", "parallel"),
        name="pool_prompt",
    )(u, u, wbd, scale)


def _pool_sample_kernel(ext_ref, wbd_ref, scale_ref, o_ref, *, pos0):
    t, b, width = o_ref.shape
    ext = ext_ref[...]
    sums = [s.reshape(t * b, width) for s in _window_sums(ext, t, 0)]
    u = ext[POOL_BUF:].reshape(t * b, width)
    pos = pos0 + lax.broadcasted_iota(jnp.int32, (t, b, width), 0).reshape(t * b, width)
    _pool_finish(sums, u, pos, wbd_ref, scale_ref, o_ref)


def _pool_sample(ext, wbd, scale, pos0):
    rows, b, width = ext.shape
    t = rows - POOL_BUF
    return pl.pallas_call(
        functools.partial(_pool_sample_kernel, pos0=pos0),
        out_shape=jax.ShapeDtypeStruct((t, b, width), BF16),
        compiler_params=pltpu.CompilerParams(vmem_limit_bytes=VMEM_LIMIT),
        name="pool_sample",
    )(ext, wbd, scale)


def _merge_kernel(x_ref, sb_ref, pool_ref, mem_ref, g_ref, wg_ref, bg_ref,
                  wsb_ref, wpool_ref, wmem_ref, wout_ref, h_ref):
    d = x_ref.shape[1]
    x = x_ref[...]
    xn = _rmsnorm(x, g_ref[...]).astype(BF16)
    merged = None
    for j, (o_ref, w_ref) in enumerate(((sb_ref, wsb_ref), (pool_ref, wpool_ref), (mem_ref, wmem_ref))):
        gate = jax.nn.sigmoid(_dot(xn, wg_ref[:, j * d:(j + 1) * d]) + bg_ref[:, j * d:(j + 1) * d])
        term = gate * _dot(o_ref[...], w_ref[...])
        merged = term if merged is None else merged + term
    h_ref[...] = x + _dot(merged.astype(BF16), wout_ref[...])


def _merge(x, o_sb, o_pool, o_mem, g, w_gate, b_gate, w_sb, w_pool, w_mem, w_out, tm):
    rows, d = x.shape

    def rows_spec(a):
        return pl.BlockSpec((tm, a.shape[1]), lambda i: (i, 0))

    g2, bg2 = g.reshape(1, d), b_gate.reshape(1, -1)
    consts = (g2, w_gate, bg2, w_sb, w_pool, w_mem, w_out)
    return pl.pallas_call(
        _merge_kernel,
        grid=(rows // tm,),
        in_specs=[rows_spec(a) for a in (x, o_sb, o_pool, o_mem)] + [_const_spec(a, 1) for a in consts],
        out_specs=pl.BlockSpec((tm, d), lambda i: (i, 0)),
        out_shape=jax.ShapeDtypeStruct((rows, d), F32),
        compiler_params=_params("parallel"),
        name="merge",
    )(x, o_sb, o_pool, o_mem, *consts)


def _conv_ffn_kernel(h_ref, hist0_ref, g_ref, wup_ref, cw_ref, cb_ref, wdown_ref, gfin_ref,
                     y_ref, hist_ref, hist_sc, act_sc, *, shift, final_norm):
    tm = h_ref.shape[1]
    hr = hist_sc.shape[0]
    d_ff = wdown_ref.shape[0]
    step = pl.program_id(1)

    @pl.when(step == 0)
    def _load_history():
        hist_sc[...] = hist0_ref[0]

    h = h_ref[0]
    hn = _rmsnorm(h, g_ref[...]).astype(BF16)
    for c in range(d_ff // MXU_COLS):
        halves = []
        for base in (c * MXU_COLS, d_ff + c * MXU_COLS):
            cols = slice(base, base + MXU_COLS)
            up = _dot(hn, wup_ref[:, cols])
            ext = jnp.concatenate([hist_sc[:, cols], up], axis=0)
            conv = cb_ref[:, cols]
            for j in range(CONV_WIDTH):
                lo = hr - (CONV_WIDTH - 1 - j) * shift
                conv = conv + ext[lo:lo + tm] * cw_ref[j:j + 1, cols]
            hist_sc[:, cols] = ext[tm:tm + hr]
            halves.append(conv)
        gate = halves[0]
        act = 0.5 * gate * (1.0 + lax.erf(gate * (2.0 ** -0.5))) * halves[1]
        act_sc[:, c * MXU_COLS:(c + 1) * MXU_COLS] = act.astype(BF16)
    y = h + _dot(act_sc[...], wdown_ref[...])
    if final_norm:
        y = _rmsnorm(y, gfin_ref[...])
    y_ref[0] = y

    @pl.when(step == pl.num_programs(1) - 1)
    def _store_history():
        hist_ref[0] = hist_sc[...]


def _conv_ffn(h, hist0, g, w_up, conv_w, conv_b, w_down, g_final, tm, shift, final_norm):
    b, t, d = h.shape
    hr, f = hist0.shape[1], hist0.shape[2]
    d_ff = w_down.shape[0]
    consts = (g.reshape(1, d), w_up, conv_w, conv_b.reshape(1, f), w_down, g_final.reshape(1, d))
    return pl.pallas_call(
        functools.partial(_conv_ffn_kernel, shift=shift, final_norm=final_norm),
        grid=(b, t // tm),
        in_specs=[pl.BlockSpec((1, tm, d), lambda bi, i: (bi, i, 0)),
                  pl.BlockSpec((1, hr, f), lambda bi, i: (bi, 0, 0))] + [_const_spec(a, 2) for a in consts],
        out_specs=[pl.BlockSpec((1, tm, d), lambda bi, i: (bi, i, 0)),
                   pl.BlockSpec((1, hr, f), lambda bi, i: (bi, 0, 0))],
        out_shape=[jax.ShapeDtypeStruct((b, t, d), F32), jax.ShapeDtypeStruct((b, hr, f), F32)],
        scratch_shapes=[pltpu.VMEM((hr, f), F32), pltpu.VMEM((tm, d_ff), BF16)],
        compiler_params=_params("parallel", "arbitrary"),
        name="conv_ffn",
    )(h, hist0, *consts)


def _bias_rows(bias, rows_per_head, heads_per_stack):
    rows = jnp.repeat(bias.astype(F32), rows_per_head).reshape(-1, heads_per_stack * rows_per_head, 1)
    return jnp.broadcast_to(rows, rows.shape[:2] + (LANES,))


def _block_diag(w):
    groups, c, _ = w.shape
    eye = jnp.eye(groups, dtype=w.dtype)
    return (eye[:, None, :, None] * w[:, :, None, :]).reshape(groups * c, groups * c)


def _heads_last(xt, heads):
    lead, n = xt.shape[:-2], xt.shape[-1]
    xt = xt.reshape(*lead, heads, HEAD_DIM, n)
    return jnp.moveaxis(xt, -1, -3)


def _heads_first_t(x):
    b, n, heads, hd = x.shape
    return x.transpose(0, 2, 3, 1).reshape(b, heads * hd, n)


def kernel(x_prompt, x_sample, cache_k, cache_v, cache_mem_k, cache_mem_v, state_pool, state_ffn, page_table, mem_prompt, g_mix, w_in, sb_bias, w_gate, b_gate, w_pool, pool_scale, w_br_sb, w_br_pool, w_br_mem, w_out, g_mem, w_mem_kv, g_ffn, w_up, conv_w, conv_b, w_down, g_final):
    nb, seq, d = x_prompt.shape
    db, dseq, _ = x_sample.shape
    depth = w_in.shape[0]
    sb_w = w_br_sb.shape[1]
    pool_w = w_br_pool.shape[1]
    mem_w = w_br_mem.shape[1]
    f = w_up.shape[2]
    sb_heads = sb_w // HEAD_DIM
    mem_heads = mem_w // HEAD_DIM
    past_len = page_table.shape[1] * cache_k.shape[2]
    sb_scale = HEAD_DIM ** -0.5
    mem_scale = HEAD_DIM ** -0.5
    assert cache_k.shape[2] == PAGE and seq % (2 * PROMPT_BLOCK) == 0
    assert seq >= POOL_BUF and dseq >= CONV_WIDTH - 1

    hp, hs = x_prompt, x_sample
    outs = [[] for _ in range(10)]
    for l in range(depth):
        last = l == depth - 1
        w_in_l = w_in[l].astype(BF16)
        o_q, o_k, o_v, o_u, o_m = 0, sb_w, 2 * sb_w, 3 * sb_w, 3 * sb_w + pool_w
        w_kv_t = w_in_l[:, o_k:o_u].T
        wbd = _block_diag(w_pool[l]).astype(BF16)
        scale = pool_scale[l].reshape(1, pool_w)
        merge_w = (g_mix[l], w_gate[l].astype(BF16), b_gate[l], w_br_sb[l].astype(BF16),
                   w_br_pool[l].astype(BF16), w_br_mem[l].astype(BF16), w_out[l].astype(BF16))
        ffn_w = (g_ffn[l], w_up[l].astype(BF16), conv_w[l], conv_b[l], w_down[l].astype(BF16), g_final)

        mkt, mvt = _norm_proj(mem_prompt, g_mem[l], None, (), w_mem_kv[l].astype(BF16).T,
                              (mem_w, mem_w), False, tm=mem_prompt.shape[1])
        q, u, qm, kt, ktb, vt, vtb = _norm_proj(
            hp, g_mix[l], w_in_l,
            ((o_q, sb_w, BF16, sb_scale), (o_u, pool_w, F32, 1.0), (o_m, mem_w, F32, mem_scale)),
            w_kv_t, (sb_w, sb_w), True, tm=_row_tile(seq))
        q_s, k_s, v_s, u_s, qm_s = (a.reshape(db, dseq, -1) for a in _norm_proj(
            hs.reshape(1, db * dseq, d), g_mix[l], w_in_l,
            ((o_q, sb_w, F32, sb_scale), (o_k, sb_w, F32, 1.0), (o_v, sb_w, F32, 1.0),
             (o_u, pool_w, F32, 1.0), (o_m, mem_w, F32, mem_scale)), None, (), False,
            tm=_row_tile(db * dseq)))
        o_sb_s, o_sb = _sb_attention(
            q_s, k_s, v_s, _heads_first_t(cache_k[l]), _heads_first_t(cache_v[l]), page_table,
            _bias_rows(sb_bias[l], dseq, sb_heads)[0], _sums_matrix(PAGE),
            q, ktb, vtb, _bias_rows(sb_bias[l], PROMPT_BLOCK, 2), _sums_matrix(PROMPT_BLOCK),
            pages_per_step=16, pairs_per_item=sb_w // LANES)
        o_pool = _pool_prompt(u, wbd, scale, tm=_row_tile(seq))
        o_mem = _mem_attn(qm, mkt, mvt, tm=_row_tile(seq, ROW_TILE // 2), nb=1)
        h = _merge(hp.reshape(nb * seq, d), o_sb.reshape(-1, sb_w), o_pool.reshape(-1, pool_w),
                   o_mem.reshape(-1, mem_w), *merge_w, tm=_row_tile(nb * seq))
        hist_rows = 8
        hp, hist = _conv_ffn(h.reshape(nb, seq, d), jnp.zeros((nb, hist_rows, f), F32), *ffn_w,
                             tm=_row_tile(seq), shift=1, final_norm=last)
        for i, val in zip((0, 1, 4, 5, 6, 8),
                          (_heads_last(kt, sb_heads), _heads_last(vt, sb_heads),
                           _heads_last(mkt, mem_heads), _heads_last(mvt, mem_heads),
                           u[:, seq - POOL_BUF:], hist[:, hist_rows - (CONV_WIDTH - 1):])):
            outs[i].append(val)

        o_sb, k, v, u, qm = o_sb_s, k_s, v_s, u_s, qm_s
        ext = jnp.concatenate([state_pool[l].transpose(1, 0, 2), u.transpose(1, 0, 2)], axis=0)
        o_pool = _pool_sample(ext, wbd, scale, past_len).transpose(1, 0, 2)
        o_mem = _mem_attn(qm, _heads_first_t(cache_mem_k[l]), _heads_first_t(cache_mem_v[l]), tm=dseq, nb=8)
        h = _merge(hs.reshape(db * dseq, d), o_sb.reshape(-1, sb_w), o_pool.reshape(-1, pool_w),
                   o_mem.reshape(-1, mem_w), *merge_w, tm=_row_tile(db * dseq))
        h_t = h.reshape(db, dseq, d).transpose(1, 0, 2).reshape(1, dseq * db, d)
        hist0 = state_ffn[l].transpose(1, 0, 2).reshape(1, (CONV_WIDTH - 1) * db, f)
        y, hist = _conv_ffn(h_t, hist0, *ffn_w, tm=dseq * db, shift=db, final_norm=last)
        hs = y.reshape(dseq, db, d).transpose(1, 0, 2)
        for i, val in zip((2, 3, 7, 9),
                          (k.reshape(db, dseq, sb_heads, HEAD_DIM), v.reshape(db, dseq, sb_heads, HEAD_DIM),
                           ext[ext.shape[0] - POOL_BUF:].transpose(1, 0, 2),
                           hist.reshape(CONV_WIDTH - 1, db, f).transpose(1, 0, 2))):
            outs[i].append(val)

    k_p, v_p, k_s, v_s, mk_p, mv_p, pool_p, pool_s, ffn_p, ffn_s = (jnp.stack(o) for o in outs)
    return (hp, hs, k_p, v_p, k_s, v_s, mk_p, mv_p, pool_p, pool_s, ffn_p, ffn_s)
```

```python
import functools

import jax
import jax.numpy as jnp
import numpy as np
from jax import lax
from jax.experimental import pallas as pl
from jax.experimental.pallas import tpu as pltpu

F32 = jnp.float32
BF16 = jnp.bfloat16

NORM_EPS = 1e-6
LANES = 128
MXU_COLS = 256
HEAD_DIM = 64
POOL_WINDOWS = (2, 4, 8, 16)
POOL_BUF = max(POOL_WINDOWS) - 1
CONV_WIDTH = 3
PAGE = 128
PROMPT_BLOCK = 256
VMEM_LIMIT = 56 * 1024 * 1024
ROW_TILE = 1024


def _row_tile(rows, tile=ROW_TILE):
    tile = min(tile, rows)
    assert rows % tile == 0
    return tile


def _params(*sem):
    return pltpu.CompilerParams(dimension_semantics=sem, vmem_limit_bytes=VMEM_LIMIT)


def _rmsnorm(x, g):
    ms = jnp.mean(x * x, axis=-1, keepdims=True)
    return x * lax.rsqrt(ms + NORM_EPS) * g


def _div_pow2(x, n):
    assert n & (n - 1) == 0
    return lax.shift_right_logical(x, n.bit_length() - 1)


def _head_of(lane):
    return _div_pow2(lane, HEAD_DIM)


def _const_spec(a, grid_rank):
    index = {1: lambda i: (0,) * a.ndim, 2: lambda i, j: (0,) * a.ndim}[grid_rank]
    return pl.BlockSpec(a.shape, index, pipeline_mode=pl.Buffered(1))


def _dot(a, b):
    return jnp.dot(a, b, preferred_element_type=F32)


def _dot_nt(a, b):
    return lax.dot_general(a, b, (((1,), (1,)), ((), ())), preferred_element_type=F32)


def _norm_proj_kernel(*refs, cols, t_rows, blocked):
    refs = list(refs)
    x_ref, g_ref = refs.pop(0), refs.pop(0)
    w_ref = refs.pop(0) if cols else None
    wt_ref = refs.pop(0) if t_rows else None
    xn = _rmsnorm(x_ref[0], g_ref[...]).astype(BF16)
    for off, width, scale in cols:
        o_ref = refs.pop(0)
        y = _dot(xn, w_ref[:, off:off + width])
        if scale != 1.0:
            y = y * scale
        o_ref[0] = y.astype(o_ref.dtype)
    if t_rows:
        yt = _dot_nt(wt_ref[...], xn)
        off = 0
        for rows in t_rows:
            refs.pop(0)[0] = yt[off:off + rows]
            if blocked:
                b_ref = refs.pop(0)
                for r in range(b_ref.shape[1]):
                    for c in range(b_ref.shape[2]):
                        b_ref[0, r, c] = yt[off + r * LANES:off + (r + 1) * LANES,
                                            c * PROMPT_BLOCK:(c + 1) * PROMPT_BLOCK].astype(b_ref.dtype)
            off += rows


def _norm_proj(x, g, w, cols, wt, t_rows, blocked, tm):
    b, t, d = x.shape
    kern_cols = tuple((off, width, scale) for off, width, _, scale in cols)
    args, in_specs = [x, g.reshape(1, d)], [pl.BlockSpec((1, tm, d), lambda bi, i: (bi, i, 0)),
                                           _const_spec(g.reshape(1, d), 2)]
    for a in ((w,) if cols else ()) + ((wt,) if t_rows else ()):
        args.append(a)
        in_specs.append(_const_spec(a, 2))
    out_specs = [pl.BlockSpec((1, tm, width), lambda bi, i: (bi, i, 0)) for _, width, _, _ in cols]
    out_shape = [jax.ShapeDtypeStruct((b, t, width), dt) for _, width, dt, _ in cols]
    for rows in t_rows:
        out_specs.append(pl.BlockSpec((1, rows, tm), lambda bi, i: (bi, 0, i)))
        out_shape.append(jax.ShapeDtypeStruct((b, rows, t), F32))
        if blocked:
            nr, nc = rows // LANES, tm // PROMPT_BLOCK
            out_specs.append(pl.BlockSpec((1, nr, nc, LANES, PROMPT_BLOCK), lambda bi, i: (bi, 0, i, 0, 0)))
            out_shape.append(jax.ShapeDtypeStruct((b, nr, t // PROMPT_BLOCK, LANES, PROMPT_BLOCK), BF16))
    return pl.pallas_call(
        functools.partial(_norm_proj_kernel, cols=kern_cols, t_rows=tuple(t_rows), blocked=blocked),
        grid=(b, t // tm),
        in_specs=in_specs, out_specs=out_specs, out_shape=out_shape,
        compiler_params=_params("parallel", "parallel"),
        name="norm_proj",
    )(*args)


def _sb_weights(z, carry, sums, mask):
    n = z.shape[1]
    drop = _softplus(z)
    if mask is not None:
        drop = jnp.where(mask, drop, 0.0)
    summed = _dot(drop.astype(BF16), sums)
    a = jnp.exp((z - drop) - summed[:, :n] - _lanes(carry, n))
    if mask is not None:
        a = jnp.where(mask, a, 0.0)
    return a.astype(BF16), carry + summed[:, n:]


def _lanes(x, n):
    return x if n == LANES else jnp.concatenate([x] * (n // LANES), axis=1)


def _softplus(z):
    return jnp.maximum(z, 0.0) + jnp.log(1.0 + jnp.exp(-jnp.abs(z)))


def _sums_matrix(n):
    r = lax.broadcasted_iota(jnp.int32, (n, n + LANES), 0)
    c = lax.broadcasted_iota(jnp.int32, (n, n + LANES), 1)
    return jnp.where((c >= n) | (r > c), 1.0, 0.0).astype(BF16)


def _prompt_item(first, last, group, q_ref, kt_ref, vt_ref, bias_ref, tri_ref, o_ref, acc_sc, carry_sc):
    tq = q_ref.shape[1]
    pairs = q_ref.shape[2] // LANES
    lane = lax.broadcasted_iota(jnp.int32, (tq, LANES), 1)
    tri = tri_ref[...]

    def run(mask, reset):
        for p in range(pairs):
            q2 = q_ref[0, :, p * LANES:(p + 1) * LANES]
            zero = jnp.zeros_like(q2)
            qbd = jnp.concatenate([jnp.where(lane < HEAD_DIM, q2, zero),
                                   jnp.where(lane >= HEAD_DIM, q2, zero)], axis=0)
            z = _dot(qbd, kt_ref[0, p, 0]) + _lanes(bias_ref[group * pairs + p], tq)
            carry = jnp.zeros((2 * tq, LANES), F32) if reset else carry_sc[p]
            a, carry = _sb_weights(z, carry, tri, mask)
            o = _dot_nt(a, vt_ref[0, p, 0])
            acc_sc[p] = o if reset else acc_sc[p] + o
            carry_sc[p] = carry

    @pl.when(first)
    def _diagonal():
        row = lax.broadcasted_iota(jnp.int32, (2 * tq, tq), 0)
        col = lax.broadcasted_iota(jnp.int32, (2 * tq, tq), 1)
        run(col < jnp.where(row >= tq, row - tq, row), True)

    @pl.when(jnp.logical_not(first))
    def _earlier():
        run(None, False)

    @pl.when(last)
    def _store():
        for p in range(pairs):
            o_ref[0, :, p * LANES:(p + 1) * LANES] = jnp.where(
                lane < HEAD_DIM, acc_sc[p, :tq], acc_sc[p, tq:]).astype(o_ref.dtype)


SCHED_FIELDS = 8


def _prompt_schedule(nb, groups, nq, n_steps):
    items = [(b, hg, qi, kb) for b in range(nb) for hg in range(groups)
             for qi in range(nq) for kb in range(qi, -1, -1)]
    assert 0 < len(items) <= n_steps, "needs at least as many sample steps as prompt items"
    sched = np.zeros((n_steps, SCHED_FIELDS), np.int32)
    for i, (b, hg, qi, kb) in enumerate(items):
        sched[(i * n_steps) // len(items)] = (1, kb == qi, kb == 0, b, hg, qi, kb, 0)
    for s in range(1, n_steps):
        if not sched[s, 0]:
            sched[s, 3:] = sched[s - 1, 3:]
    return jnp.asarray(sched.reshape(-1))


PAGE_RING = 3


def _sb_kernel(pt_ref, sched_ref, tok_ref, bias_ref, tri_ref, ck_hbm, cv_hbm,
               pq_ref, pkt_ref, pvt_ref, pbias_ref, ptri_ref, o_ref, po_ref,
               kbuf, vbuf, sems, qbd_sc, carry_sc, acc_new_sc, acc_t_sc, pacc_sc, pcarry_sc,
               *, pages_per_step, n_pages):
    n = pages_per_step
    steps = n_pages // n
    tq, width = tok_ref.shape[2], tok_ref.shape[3]
    heads = width // HEAD_DIM
    m = heads * tq
    bi, step = pl.program_id(0), pl.program_id(1)
    g = bi * steps + step
    n_steps = pl.num_programs(0) * steps
    slot = lax.rem(g, PAGE_RING)
    bias = bias_ref[...]
    tri = tri_ref[...]

    def page_copies(gi, buf):
        b_idx = lax.div(gi, steps)
        first_page = 2 * b_idx * n_pages + n_pages - 1 - gi * n
        copies = []
        for j in range(n):
            page = pt_ref[first_page - j]
            copies.append(pltpu.make_async_copy(ck_hbm.at[page], kbuf.at[buf, j], sems.at[0, buf]))
            copies.append(pltpu.make_async_copy(cv_hbm.at[page], vbuf.at[buf, j], sems.at[1, buf]))
        return copies

    def start_all(copies):
        for i, c in enumerate(copies):
            c.start(priority=i % 2)

    @pl.when(g == 0)
    def _first_fetches():
        for ahead in range(PAGE_RING - 1):
            @pl.when(ahead < n_steps)
            def _():
                start_all(page_copies(g + ahead, ahead))

    ahead = g + PAGE_RING - 1

    @pl.when(ahead < n_steps)
    def _prefetch():
        start_all(page_copies(ahead, lax.rem(ahead, PAGE_RING)))

    item = g * SCHED_FIELDS

    @pl.when(sched_ref[item] == 1)
    def _prompt():
        _prompt_item(sched_ref[item + 1] == 1, sched_ref[item + 2] == 1, sched_ref[item + 4],
                     pq_ref, pkt_ref, pvt_ref, pbias_ref, ptri_ref, po_ref, pacc_sc, pcarry_sc)

    @pl.when(step == 0)
    def _new_tokens():
        qt = jnp.concatenate([tok_ref[0, 0]] * heads, axis=0)
        row = lax.broadcasted_iota(jnp.int32, (m, width), 0)
        lane = lax.broadcasted_iota(jnp.int32, (m, width), 1)
        qbd = jnp.where(_div_pow2(row, tq) == _head_of(lane), qt, 0.0).astype(BF16)
        qbd_sc[...] = qbd
        pad = jnp.zeros((PAGE - tq, width), F32)
        kn = jnp.concatenate([tok_ref[1, 0], pad], axis=0).astype(BF16)
        vn = jnp.concatenate([tok_ref[2, 0], pad], axis=0).astype(BF16)
        r = lax.broadcasted_iota(jnp.int32, (m, PAGE), 0)
        c = lax.broadcasted_iota(jnp.int32, (m, PAGE), 1)
        earlier_token = c < r - tq * _div_pow2(r, tq)
        a, carry = _sb_weights(_dot_nt(qbd, kn) + bias, jnp.zeros((m, LANES), F32), tri, earlier_token)
        acc_new_sc[...] = _dot(a, vn)
        acc_t_sc[...] = jnp.zeros_like(acc_t_sc)
        carry_sc[...] = carry

    for c in page_copies(g, slot):
        c.wait()

    kt = jnp.concatenate([kbuf[slot, j] for j in range(n)], axis=1).astype(BF16)
    z = _dot(qbd_sc[...], kt) + _lanes(bias, n * PAGE)
    drop = _softplus(z)
    stacked = jnp.concatenate([drop[:, j * PAGE:(j + 1) * PAGE] for j in range(n)], axis=0)
    summed = _dot(stacked.astype(BF16), tri)
    own = z - drop
    carry, weights = carry_sc[...], []
    for j in range(n):
        rows = slice(j * m, (j + 1) * m)
        weights.append(jnp.exp(own[:, j * PAGE:(j + 1) * PAGE] - summed[rows, :PAGE] - carry))
        carry = carry + summed[rows, PAGE:]
    carry_sc[...] = carry
    a = jnp.concatenate(weights, axis=1)
    a = jnp.concatenate([a, jnp.zeros((LANES - m, n * PAGE), F32)], axis=0)
    vt = jnp.concatenate([vbuf[slot, j] for j in range(n)], axis=1).astype(BF16)
    acc_t_sc[...] += _dot(vt, a.T.astype(BF16))

    @pl.when(step == steps - 1)
    def _finish():
        acc = acc_new_sc[...] + acc_t_sc[...].T[:m]
        lane = lax.broadcasted_iota(jnp.int32, (tq, LANES), 1)
        pieces = []
        for j in range(width // LANES):
            lo = acc[2 * j * tq:(2 * j + 1) * tq, j * LANES:(j + 1) * LANES]
            hi = acc[(2 * j + 1) * tq:(2 * j + 2) * tq, j * LANES:(j + 1) * LANES]
            pieces.append(jnp.where(lane < HEAD_DIM, lo, hi))
        o_ref[0] = jnp.concatenate(pieces, axis=1).astype(o_ref.dtype)


def _sb_attention(q, k_new, v_new, cache_kt, cache_vt, page_table, bias_rows, tri,
                  pq, pkt, pvt, pbias_rows, ptri, pages_per_step, pairs_per_item):
    b, tq, width = q.shape
    nb, t, _ = pq.shape
    n_pages = page_table.shape[1]
    m = (width // HEAD_DIM) * tq
    pp, blk = pairs_per_item, PROMPT_BLOCK
    assert n_pages % pages_per_step == 0 and m <= LANES and width % (pp * LANES) == 0
    steps = n_pages // pages_per_step
    sched = _prompt_schedule(nb, width // (pp * LANES), t // blk, b * steps)

    def field(k):
        return lambda bi, s, pt, sc: sc[(bi * steps + s) * SCHED_FIELDS + k]

    pb, hg, qi, kb = field(3), field(4), field(5), field(6)
    tok = pl.BlockSpec((1, tq, width), lambda bi, s, pt, sc: (bi, 0, 0))
    pkv = pl.BlockSpec((1, pp, 1, LANES, blk), lambda *a: (pb(*a), hg(*a), kb(*a), 0, 0))
    pqo = pl.BlockSpec((1, blk, pp * LANES), lambda *a: (pb(*a), qi(*a), hg(*a)))
    page_buffers = pltpu.VMEM((PAGE_RING, pages_per_step, width, PAGE), F32)
    whole = pl.BlockSpec(memory_space=pltpu.VMEM)
    grid_spec = pltpu.PrefetchScalarGridSpec(
        num_scalar_prefetch=2,
        grid=(b, steps),
        in_specs=[pl.BlockSpec((3, 1, tq, width), lambda bi, s, pt, sc: (0, bi, 0, 0)), whole, whole,
                  pl.BlockSpec(memory_space=pl.ANY),
                  pl.BlockSpec(memory_space=pl.ANY),
                  pqo, pkv, pkv, whole, whole],
        out_specs=[tok, pqo],
        scratch_shapes=[page_buffers, page_buffers,
                        pltpu.SemaphoreType.DMA((2, PAGE_RING)),
                        pltpu.VMEM((m, width), BF16),
                        pltpu.VMEM((m, LANES), F32),
                        pltpu.VMEM((m, width), F32),
                        pltpu.VMEM((width, LANES), F32),
                        pltpu.VMEM((pp, 2 * blk, LANES), F32),
                        pltpu.VMEM((pp, 2 * blk, LANES), F32)],
    )
    return pl.pallas_call(
        functools.partial(_sb_kernel, pages_per_step=pages_per_step, n_pages=n_pages),
        grid_spec=grid_spec,
        out_shape=[jax.ShapeDtypeStruct((b, tq, width), BF16), jax.ShapeDtypeStruct((nb, t, width), BF16)],
        compiler_params=_params("arbitrary", "arbitrary"),
        name="sb_attention",
    )(page_table.reshape(-1), sched, jnp.stack([q, k_new, v_new]), bias_rows, tri, cache_kt, cache_vt,
      pq, pkt, pvt, pbias_rows, ptri)


def _mem_attn_kernel(q_ref, kt_ref, vt_ref, o_ref):
    nb, tm, width = q_ref.shape
    heads = width // HEAD_DIM
    lane = lax.broadcasted_iota(jnp.int32, (tm, width), 1)
    for i in range(nb):
        q = q_ref[i]
        qbd = jnp.concatenate([jnp.where(_head_of(lane) == h, q, 0.0) for h in range(heads)],
                              axis=0).astype(BF16)
        s = _dot(qbd, kt_ref[i].astype(BF16))
        p = jnp.exp(s - jnp.max(s, axis=-1, keepdims=True))
        denom = jnp.sum(p, axis=-1, keepdims=True)
        o = _dot_nt(p.astype(BF16), vt_ref[i].astype(BF16)) / denom
        out = o[:tm]
        for h in range(1, heads):
            out = jnp.where(_head_of(lane) == h, o[h * tm:(h + 1) * tm], out)
        o_ref[i] = out.astype(o_ref.dtype)


def _mem_attn(q, kt, vt, tm, nb):
    b, t, width = q.shape
    mem = kt.shape[2]
    return pl.pallas_call(
        _mem_attn_kernel,
        grid=(b // nb, t // tm),
        in_specs=[pl.BlockSpec((nb, tm, width), lambda bi, i: (bi, i, 0)),
                  pl.BlockSpec((nb, width, mem), lambda bi, i: (bi, 0, 0)),
                  pl.BlockSpec((nb, width, mem), lambda bi, i: (bi, 0, 0))],
        out_specs=pl.BlockSpec((nb, tm, width), lambda bi, i: (bi, i, 0)),
        out_shape=jax.ShapeDtypeStruct((b, t, width), BF16),
        compiler_params=_params("parallel", "parallel"),
        name="mem_attn",
    )(q, kt, vt)


def _window_sums(ext, t, axis):
    def cut(x, lo, n):
        return lax.slice_in_dim(x, lo, lo + n, axis=axis)

    length = ext.shape[axis]
    sums, cur, span = [], ext, 1
    for w in POOL_WINDOWS:
        while span < w:
            n = cur.shape[axis] - span
            cur = cut(cur, span, n) + cut(cur, 0, n)
            span *= 2
        sums.append(cut(cur, length - t - w + 1, t))
    return sums


def _pool_finish(sums, u, pos, wbd_ref, scale_ref, o_ref):
    lane = lax.broadcasted_iota(jnp.int32, u.shape, 1)
    total, window = sums[0], jnp.full(u.shape, POOL_WINDOWS[0], jnp.int32)
    for g in range(1, len(POOL_WINDOWS)):
        sel = _head_of(lane) == g
        total = jnp.where(sel, sums[g], total)
        window = jnp.where(sel, POOL_WINDOWS[g], window)
    count = jnp.minimum(window, pos + 1).astype(F32)
    pooled = total / count - u
    o_ref[...] = (_dot(pooled.astype(BF16), wbd_ref[...]) * scale_ref[...]).astype(o_ref.dtype).reshape(o_ref.shape)


def _pool_prompt_kernel(u_ref, halo_ref, wbd_ref, scale_ref, o_ref):
    tm = u_ref.shape[1]
    i = pl.program_id(1)
    u = u_ref[0]
    halo = jnp.where(i > 0, halo_ref[0], 0.0)
    ext = jnp.concatenate([halo, u], axis=0)
    pos = i * tm + lax.broadcasted_iota(jnp.int32, u.shape, 0)
    _pool_finish(_window_sums(ext, tm, 0), u, pos, wbd_ref, scale_ref, o_ref)


def _pool_prompt(u, wbd, scale, tm):
    b, t, width = u.shape
    halo = 16
    return pl.pallas_call(
        _pool_prompt_kernel,
        grid=(b, t // tm),
        in_specs=[pl.BlockSpec((1, tm, width), lambda bi, i: (bi, i, 0)),
                  pl.BlockSpec((1, halo, width), lambda bi, i: (bi, jnp.maximum(i * (tm // halo) - 1, 0), 0)),
                  pl.BlockSpec((width, width), lambda bi, i: (0, 0)),
                  pl.BlockSpec((1, width), lambda bi, i: (0, 0))],
        out_specs=pl.BlockSpec((1, tm, width), lambda bi, i: (bi, i, 0)),
        out_shape=jax.ShapeDtypeStruct((b, t, width), BF16),
        compiler_params=_params("parallel", "parallel"),
        name="pool_prompt",
    )(u, u, wbd, scale)


def _pool_sample_kernel(ext_ref, wbd_ref, scale_ref, o_ref, *, pos0):
    t, b, width = o_ref.shape
    ext = ext_ref[...]
    sums = [s.reshape(t * b, width) for s in _window_sums(ext, t, 0)]
    u = ext[POOL_BUF:].reshape(t * b, width)
    pos = pos0 + lax.broadcasted_iota(jnp.int32, (t, b, width), 0).reshape(t * b, width)
    _pool_finish(sums, u, pos, wbd_ref, scale_ref, o_ref)


def _pool_sample(ext, wbd, scale, pos0):
    rows, b, width = ext.shape
    t = rows - POOL_BUF
    return pl.pallas_call(
        functools.partial(_pool_sample_kernel, pos0=pos0),
        out_shape=jax.ShapeDtypeStruct((t, b, width), BF16),
        compiler_params=pltpu.CompilerParams(vmem_limit_bytes=VMEM_LIMIT),
        name="pool_sample",
    )(ext, wbd, scale)


def _merge_kernel(x_ref, sb_ref, pool_ref, mem_ref, g_ref, wg_ref, bg_ref,
                  wsb_ref, wpool_ref, wmem_ref, wout_ref, h_ref):
    d = x_ref.shape[1]
    x = x_ref[...]
    xn = _rmsnorm(x, g_ref[...]).astype(BF16)
    merged = None
    for j, (o_ref, w_ref) in enumerate(((sb_ref, wsb_ref), (pool_ref, wpool_ref), (mem_ref, wmem_ref))):
        gate = jax.nn.sigmoid(_dot(xn, wg_ref[:, j * d:(j + 1) * d]) + bg_ref[:, j * d:(j + 1) * d])
        term = gate * _dot(o_ref[...], w_ref[...])
        merged = term if merged is None else merged + term
    h_ref[...] = x + _dot(merged.astype(BF16), wout_ref[...])


def _merge(x, o_sb, o_pool, o_mem, g, w_gate, b_gate, w_sb, w_pool, w_mem, w_out, tm):
    rows, d = x.shape

    def rows_spec(a):
        return pl.BlockSpec((tm, a.shape[1]), lambda i: (i, 0))

    g2, bg2 = g.reshape(1, d), b_gate.reshape(1, -1)
    consts = (g2, w_gate, bg2, w_sb, w_pool, w_mem, w_out)
    return pl.pallas_call(
        _merge_kernel,
        grid=(rows // tm,),
        in_specs=[rows_spec(a) for a in (x, o_sb, o_pool, o_mem)] + [_const_spec(a, 1) for a in consts],
        out_specs=pl.BlockSpec((tm, d), lambda i: (i, 0)),
        out_shape=jax.ShapeDtypeStruct((rows, d), F32),
        compiler_params=_params("parallel"),
        name="merge",
    )(x, o_sb, o_pool, o_mem, *consts)


def _conv_ffn_kernel(h_ref, hist0_ref, g_ref, wup_ref, cw_ref, cb_ref, wdown_ref, gfin_ref,
                     y_ref, hist_ref, hist_sc, act_sc, *, shift, final_norm):
    tm = h_ref.shape[1]
    hr = hist_sc.shape[0]
    d_ff = wdown_ref.shape[0]
    step = pl.program_id(1)

    @pl.when(step == 0)
    def _load_history():
        hist_sc[...] = hist0_ref[0]

    h = h_ref[0]
    hn = _rmsnorm(h, g_ref[...]).astype(BF16)
    for c in range(d_ff // MXU_COLS):
        halves = []
        for base in (c * MXU_COLS, d_ff + c * MXU_COLS):
            cols = slice(base, base + MXU_COLS)
            up = _dot(hn, wup_ref[:, cols])
            ext = jnp.concatenate([hist_sc[:, cols], up], axis=0)
            conv = cb_ref[:, cols]
            for j in range(CONV_WIDTH):
                lo = hr - (CONV_WIDTH - 1 - j) * shift
                conv = conv + ext[lo:lo + tm] * cw_ref[j:j + 1, cols]
            hist_sc[:, cols] = ext[tm:tm + hr]
            halves.append(conv)
        gate = halves[0]
        act = 0.5 * gate * (1.0 + lax.erf(gate * (2.0 ** -0.5))) * halves[1]
        act_sc[:, c * MXU_COLS:(c + 1) * MXU_COLS] = act.astype(BF16)
    y = h + _dot(act_sc[...], wdown_ref[...])
    if final_norm:
        y = _rmsnorm(y, gfin_ref[...])
    y_ref[0] = y

    @pl.when(step == pl.num_programs(1) - 1)
    def _store_history():
        hist_ref[0] = hist_sc[...]


def _conv_ffn(h, hist0, g, w_up, conv_w, conv_b, w_down, g_final, tm, shift, final_norm):
    b, t, d = h.shape
    hr, f = hist0.shape[1], hist0.shape[2]
    d_ff = w_down.shape[0]
    consts = (g.reshape(1, d), w_up, conv_w, conv_b.reshape(1, f), w_down, g_final.reshape(1, d))
    return pl.pallas_call(
        functools.partial(_conv_ffn_kernel, shift=shift, final_norm=final_norm),
        grid=(b, t // tm),
        in_specs=[pl.BlockSpec((1, tm, d), lambda bi, i: (bi, i, 0)),
                  pl.BlockSpec((1, hr, f), lambda bi, i: (bi, 0, 0))] + [_const_spec(a, 2) for a in consts],
        out_specs=[pl.BlockSpec((1, tm, d), lambda bi, i: (bi, i, 0)),
                   pl.BlockSpec((1, hr, f), lambda bi, i: (bi, 0, 0))],
        out_shape=[jax.ShapeDtypeStruct((b, t, d), F32), jax.ShapeDtypeStruct((b, hr, f), F32)],
        scratch_shapes=[pltpu.VMEM((hr, f), F32), pltpu.VMEM((tm, d_ff), BF16)],
        compiler_params=_params("parallel", "arbitrary"),
        name="conv_ffn",
    )(h, hist0, *consts)


def _bias_rows(bias, rows_per_head, heads_per_stack):
    rows = jnp.repeat(bias.astype(F32), rows_per_head).reshape(-1, heads_per_stack * rows_per_head, 1)
    return jnp.broadcast_to(rows, rows.shape[:2] + (LANES,))


def _block_diag(w):
    groups, c, _ = w.shape
    eye = jnp.eye(groups, dtype=w.dtype)
    return (eye[:, None, :, None] * w[:, :, None, :]).reshape(groups * c, groups * c)


def _heads_last(xt, heads):
    lead, n = xt.shape[:-2], xt.shape[-1]
    xt = xt.reshape(*lead, heads, HEAD_DIM, n)
    return jnp.moveaxis(xt, -1, -3)


def _heads_first_t(x):
    b, n, heads, hd = x.shape
    return x.transpose(0, 2, 3, 1).reshape(b, heads * hd, n)


def kernel(x_prompt, x_sample, cache_k, cache_v, cache_mem_k, cache_mem_v, state_pool, state_ffn, page_table, mem_prompt, g_mix, w_in, sb_bias, w_gate, b_gate, w_pool, pool_scale, w_br_sb, w_br_pool, w_br_mem, w_out, g_mem, w_mem_kv, g_ffn, w_up, conv_w, conv_b, w_down, g_final):
    nb, seq, d = x_prompt.shape
    db, dseq, _ = x_sample.shape
    depth = w_in.shape[0]
    sb_w = w_br_sb.shape[1]
    pool_w = w_br_pool.shape[1]
    mem_w = w_br_mem.shape[1]
    f = w_up.shape[2]
    sb_heads = sb_w // HEAD_DIM
    mem_heads = mem_w // HEAD_DIM
    past_len = page_table.shape[1] * cache_k.shape[2]
    sb_scale = HEAD_DIM ** -0.5
    mem_scale = HEAD_DIM ** -0.5
    assert cache_k.shape[2] == PAGE and seq % (2 * PROMPT_BLOCK) == 0
    assert seq >= POOL_BUF and dseq >= CONV_WIDTH - 1

    hp, hs = x_prompt, x_sample
    outs = [[] for _ in range(10)]
    for l in range(depth):
        last = l == depth - 1
        w_in_l = w_in[l].astype(BF16)
        o_q, o_k, o_v, o_u, o_m = 0, sb_w, 2 * sb_w, 3 * sb_w, 3 * sb_w + pool_w
        w_kv_t = w_in_l[:, o_k:o_u].T
        wbd = _block_diag(w_pool[l]).astype(BF16)
        scale = pool_scale[l].reshape(1, pool_w)
        merge_w = (g_mix[l], w_gate[l].astype(BF16), b_gate[l], w_br_sb[l].astype(BF16),
                   w_br_pool[l].astype(BF16), w_br_mem[l].astype(BF16), w_out[l].astype(BF16))
        ffn_w = (g_ffn[l], w_up[l].astype(BF16), conv_w[l], conv_b[l], w_down[l].astype(BF16), g_final)

        mkt, mvt = _norm_proj(mem_prompt, g_mem[l], None, (), w_mem_kv[l].astype(BF16).T,
                              (mem_w, mem_w), False, tm=mem_prompt.shape[1])
        q, u, qm, kt, ktb, vt, vtb = _norm_proj(
            hp, g_mix[l], w_in_l,
            ((o_q, sb_w, BF16, sb_scale), (o_u, pool_w, F32, 1.0), (o_m, mem_w, F32, mem_scale)),
            w_kv_t, (sb_w, sb_w), True, tm=_row_tile(seq))
        q_s, k_s, v_s, u_s, qm_s = (a.reshape(db, dseq, -1) for a in _norm_proj(
            hs.reshape(1, db * dseq, d), g_mix[l], w_in_l,
            ((o_q, sb_w, F32, sb_scale), (o_k, sb_w, F32, 1.0), (o_v, sb_w, F32, 1.0),
             (o_u, pool_w, F32, 1.0), (o_m, mem_w, F32, mem_scale)), None, (), False,
            tm=_row_tile(db * dseq)))
        o_sb_s, o_sb = _sb_attention(
            q_s, k_s, v_s, _heads_first_t(cache_k[l]), _heads_first_t(cache_v[l]), page_table,
            _bias_rows(sb_bias[l], dseq, sb_heads)[0], _sums_matrix(PAGE),
            q, ktb, vtb, _bias_rows(sb_bias[l], PROMPT_BLOCK, 2), _sums_matrix(PROMPT_BLOCK),
            pages_per_step=16, pairs_per_item=sb_w // LANES)
        o_pool = _pool_prompt(u, wbd, scale, tm=_row_tile(seq))
        o_mem = _mem_attn(qm, mkt, mvt, tm=_row_tile(seq, ROW_TILE // 2), nb=1)
        h = _merge(hp.reshape(nb * seq, d), o_sb.reshape(-1, sb_w), o_pool.reshape(-1, pool_w),
                   o_mem.reshape(-1, mem_w), *merge_w, tm=_row_tile(nb * seq))
        hist_rows = 8
        hp, hist = _conv_ffn(h.reshape(nb, seq, d), jnp.zeros((nb, hist_rows, f), F32), *ffn_w,
                             tm=_row_tile(seq), shift=1, final_norm=last)
        for i, val in zip((0, 1, 4, 5, 6, 8),
                          (_heads_last(kt, sb_heads), _heads_last(vt, sb_heads),
                           _heads_last(mkt, mem_heads), _heads_last(mvt, mem_heads),
                           u[:, seq - POOL_BUF:], hist[:, hist_rows - (CONV_WIDTH - 1):])):
            outs[i].append(val)

        o_sb, k, v, u, qm = o_sb_s, k_s, v_s, u_s, qm_s
        ext = jnp.concatenate([state_pool[l].transpose(1, 0, 2), u.transpose(1, 0, 2)], axis=0)
        o_pool = _pool_sample(ext, wbd, scale, past_len).transpose(1, 0, 2)
        o_mem = _mem_attn(qm, _heads_first_t(cache_mem_k[l]), _heads_first_t(cache_mem_v[l]), tm=dseq, nb=8)
        h = _merge(hs.reshape(db * dseq, d), o_sb.reshape(-1, sb_w), o_pool.reshape(-1, pool_w),
                   o_mem.reshape(-1, mem_w), *merge_w, tm=_row_tile(db * dseq))
        h_t = h.reshape(db, dseq, d).transpose(1, 0, 2).reshape(1, dseq * db, d)
        hist0 = state_ffn[l].transpose(1, 0, 2).reshape(1, (CONV_WIDTH - 1) * db, f)
        y, hist = _conv_ffn(h_t, hist0, *ffn_w, tm=dseq * db, shift=db, final_norm=last)
        hs = y.reshape(dseq, db, d).transpose(1, 0, 2)
        for i, val in zip((2, 3, 7, 9),
                          (k.reshape(db, dseq, sb_heads, HEAD_DIM), v.reshape(db, dseq, sb_heads, HEAD_DIM),
                           ext[ext.shape[0] - POOL_BUF:].transpose(1, 0, 2),
                           hist.reshape(CONV_WIDTH - 1, db, f).transpose(1, 0, 2))):
            outs[i].append(val)

    k_p, v_p, k_s, v_s, mk_p, mv_p, pool_p, pool_s, ffn_p, ffn_s = (jnp.stack(o) for o in outs)
    return (hp, hs, k_p, v_p, k_s, v_s, mk_p, mv_p, pool_p, pool_s, ffn_p, ffn_s)
```
